```python
import math
import jax, jax.numpy as jnp
from jax import lax
import numpy as np

D_MODEL = 2048
BATCH = 8
SEQ = 4096
DEPTH = 2
DEC_BATCH = 4
DEC_SEQ = 8192
PAST_LEN = 128

HEAD_DIM = 128
ATTN_SCALE = HEAD_DIM ** -0.5
EPS = 1e-6
NEG = -1e30

A_WINDOWS = (128, 512, 2048)
A_DILATIONS = (1, 4, 16)
A_N_GROUPS = 3
A_HEADS = 4
A_WIDTH = A_HEADS * HEAD_DIM
A_QKV = 3 * A_N_GROUPS * A_HEADS * HEAD_DIM
ROPE_THETA = 500000.0
ROPE_DIMS = HEAD_DIM // 4

B_Q_HEADS = 8
B_KV_HEADS = 2
B_WIDTH = B_Q_HEADS * HEAD_DIM
B_QKV = (B_Q_HEADS + 2 * B_KV_HEADS) * HEAD_DIM
B_QBLK = 128
AXIAL_THETA = 10000.0
GRID_W = 64

C_GATE = 2 * D_MODEL
C_IN = A_QKV + B_QKV + C_GATE

D_FF = 5632
N_EXPERTS = 8
TOP_K = 2
D_FF_EXPERT = 7168
MOE_BLK = 256
N_DENSE = (DEPTH + 1) // 2
N_MOE = DEPTH // 2

kernel_name = "hybrid_dilated_axial_gqa_encoder"


def rms_norm(x, g):
    xf = x.astype(jnp.float32)
    y = xf * lax.rsqrt(jnp.mean(xf * xf, axis=-1, keepdims=True) + EPS)
    return (y * g.astype(jnp.float32)).astype(x.dtype)


def rope_angles(pos, dims, theta):
    inv = jnp.power(jnp.float32(theta), -jnp.arange(0, dims, 2, dtype=jnp.float32) / dims)
    return pos[:, None] * inv[None, :]


def rotate_half(x, ang):
    cos = jnp.cos(ang)[None, :, None, :]
    sin = jnp.sin(ang)[None, :, None, :]
    x1, x2 = jnp.split(x, 2, axis=-1)
    return jnp.concatenate([x1 * cos - x2 * sin, x1 * sin + x2 * cos], axis=-1)


def partial_rope(x, pos):
    ang = rope_angles(pos, ROPE_DIMS, ROPE_THETA)
    xr = rotate_half(x[..., :ROPE_DIMS].astype(jnp.float32), ang).astype(x.dtype)
    return jnp.concatenate([xr, x[..., ROPE_DIMS:]], axis=-1)


def axial_rope(x, rows):
    row = jnp.repeat(jnp.arange(rows, dtype=jnp.float32), GRID_W)
    col = jnp.tile(jnp.arange(GRID_W, dtype=jnp.float32), rows)
    half = HEAD_DIM // 2
    xf = x.astype(jnp.float32)
    xr = rotate_half(xf[..., :half], rope_angles(row, half, AXIAL_THETA))
    xc = rotate_half(xf[..., half:], rope_angles(col, half, AXIAL_THETA))
    return jnp.concatenate([xr, xc], axis=-1).astype(x.dtype)


def dilated_window_group(q, k, v, dilation, radius):
    B, S, H, dh = q.shape
    L = S // dilation

    def to_sub(a):
        return a.reshape(B, L, dilation, H, dh).transpose(0, 2, 1, 3, 4)

    qs, ks, vs = to_sub(q), to_sub(k), to_sub(v)
    blk = radius
    nb = -(-L // blk)
    Lp = nb * blk
    pad_q = Lp - L
    qs = jnp.pad(qs, ((0, 0), (0, 0), (0, pad_q), (0, 0), (0, 0)))
    kv_pad = ((0, 0), (0, 0), (blk, pad_q + blk), (0, 0), (0, 0))
    ks = jnp.pad(ks, kv_pad)
    vs = jnp.pad(vs, kv_pad)
    qb = qs.reshape(B, dilation, nb, blk, H, dh)
    kb = ks.reshape(B, dilation, nb + 2, blk, H, dh)
    vb = vs.reshape(B, dilation, nb + 2, blk, H, dh)
    kw = jnp.concatenate([kb[:, :, :-2], kb[:, :, 1:-1], kb[:, :, 2:]], axis=3)
    vw = jnp.concatenate([vb[:, :, :-2], vb[:, :, 1:-1], vb[:, :, 2:]], axis=3).astype(jnp.float32)
    s = jnp.einsum('brnqhd,brnkhd->brnhqk', qb, kw, preferred_element_type=jnp.float32) * ATTN_SCALE
    a_idx = jnp.arange(blk)[:, None]
    c_idx = jnp.arange(3 * blk)[None, :]
    rel = c_idx - blk - a_idx
    kpos = jnp.arange(nb)[:, None, None] * blk - blk + c_idx[None]
    valid = (jnp.abs(rel)[None] <= radius) & (kpos >= 0) & (kpos < L)
    s = jnp.where(valid[None, None, :, None], s, NEG)
    m = jnp.max(s, axis=-1, keepdims=True)
    p = jnp.exp(s - m)
    den = jnp.sum(p, axis=-1, keepdims=True)
    o = jnp.einsum('brnhqk,brnkhd->brnhqd', p, vw) / den
    lse = (m + jnp.log(den))[..., 0]
    o = o.transpose(0, 1, 2, 4, 3, 5).reshape(B, dilation, Lp, H, dh)[:, :, :L]
    o = o.transpose(0, 2, 1, 3, 4).reshape(B, S, H, dh)
    lse = lse.transpose(0, 1, 2, 4, 3).reshape(B, dilation, Lp, H)[:, :, :L]
    lse = lse.transpose(0, 2, 1, 3).reshape(B, S, H)
    return o, lse


def dilated_mixture_attention(qkv_a):
    B, S, _ = qkv_a.shape
    qkv = qkv_a.reshape(B, S, 3, A_N_GROUPS, A_HEADS, HEAD_DIM)
    pos = jnp.arange(S, dtype=jnp.float32)
    outs, lses = [], []
    for g in range(A_N_GROUPS):
        w, d = A_WINDOWS[g], A_DILATIONS[g]
        q = partial_rope(qkv[:, :, 0, g], pos)
        k = partial_rope(qkv[:, :, 1, g], pos)
        v = qkv[:, :, 2, g]
        o, lse = dilated_window_group(q, k, v, d, w // (2 * d))
        outs.append(o)
        lses.append(lse)
    wts = jax.nn.softmax(jnp.stack(lses, axis=0), axis=0)
    o = wts[0][..., None] * outs[0]
    for g in range(1, A_N_GROUPS):
        o = o + wts[g][..., None] * outs[g]
    return o.reshape(B, S, A_WIDTH).astype(qkv_a.dtype)


def axial_gqa_attention(qkv_b, q_norm_g, k_norm_g):
    B, S, _ = qkv_b.shape
    rows = S // GRID_W
    nq_cols = B_Q_HEADS * HEAD_DIM
    nk_cols = B_KV_HEADS * HEAD_DIM
    q = qkv_b[..., :nq_cols].reshape(B, S, B_Q_HEADS, HEAD_DIM)
    k = qkv_b[..., nq_cols:nq_cols + nk_cols].reshape(B, S, B_KV_HEADS, HEAD_DIM)
    v = qkv_b[..., nq_cols + nk_cols:].reshape(B, S, B_KV_HEADS, HEAD_DIM).astype(jnp.float32)
    q = axial_rope(rms_norm(q, q_norm_g), rows)
    k = axial_rope(rms_norm(k, k_norm_g), rows)
    grp = B_Q_HEADS // B_KV_HEADS
    nblk = S // B_QBLK
    qb = q.reshape(B, nblk, B_QBLK, B_KV_HEADS, grp, HEAD_DIM).transpose(1, 0, 2, 3, 4, 5)

    def query_block(qi):
        s = jnp.einsum('bqkgd,bskd->bkgqs', qi, k, preferred_element_type=jnp.float32) * ATTN_SCALE
        p = jax.nn.softmax(s, axis=-1)
        return jnp.einsum('bkgqs,bskd->bqkgd', p, v)

    o = lax.map(query_block, qb)
    return o.transpose(1, 0, 2, 3, 4, 5).reshape(B, S, B_WIDTH).astype(qkv_b.dtype)


def gated_mixer_layer(x, g_norm, w_in, b_gate, q_norm_g, k_norm_g, w_proj_a, w_proj_b, w_out):
    xn = rms_norm(x, g_norm)
    h = xn @ w_in
    qkv_a = h[..., :A_QKV]
    qkv_b = h[..., A_QKV:A_QKV + B_QKV]
    gates = jax.nn.sigmoid((h[..., A_QKV + B_QKV:] + b_gate).astype(jnp.float32)).astype(x.dtype)
    o_a = dilated_mixture_attention(qkv_a)
    o_b = axial_gqa_attention(qkv_b, q_norm_g, k_norm_g)
    merged = gates[..., :D_MODEL] * (o_a @ w_proj_a) + gates[..., D_MODEL:] * (o_b @ w_proj_b)
    return x + merged @ w_out


def swiglu(x, w_gate, w_up, w_down):
    return (jax.nn.silu(x @ w_gate) * (x @ w_up)) @ w_down


def moe_swiglu(x, w_router, w_gate, w_up, w_down):
    B, S, D = x.shape
    xt = x.reshape(-1, D)
    T = xt.shape[0]
    logits = jnp.einsum('td,de->te', xt, w_router, preferred_element_type=jnp.float32)
    top_val, top_idx = lax.top_k(logits, TOP_K)
    top_w = jax.nn.softmax(top_val, axis=-1)
    flat_e = top_idx.reshape(-1)
    flat_tok = jnp.repeat(jnp.arange(T, dtype=jnp.int32), TOP_K)
    flat_w = top_w.reshape(-1)
    order = jnp.argsort(flat_e)
    e_sorted = flat_e[order]
    counts = jnp.bincount(flat_e, length=N_EXPERTS)
    padded = (counts + MOE_BLK - 1) // MOE_BLK * MOE_BLK
    padded_end = jnp.cumsum(padded)
    padded_start = padded_end - padded
    start = jnp.cumsum(counts) - counts
    rank = jnp.arange(T * TOP_K) - start[e_sorted]
    slot = padded_start[e_sorted] + rank
    n_slots = T * TOP_K + N_EXPERTS * MOE_BLK
    n_blocks = n_slots // MOE_BLK
    slot_tok = jnp.zeros((n_slots,), jnp.int32).at[slot].set(flat_tok[order])
    slot_w = jnp.zeros((n_slots,), jnp.float32).at[slot].set(flat_w[order])
    blk_expert = jnp.minimum(
        jnp.searchsorted(padded_end, jnp.arange(n_blocks) * MOE_BLK, side='right'), N_EXPERTS - 1)
    xs = xt[slot_tok].reshape(n_blocks, MOE_BLK, D)

    def expert_block(args):
        xb, e = args
        return swiglu(xb, w_gate[e], w_up[e], w_down[e])

    ys = lax.map(expert_block, (xs, blk_expert)).reshape(n_slots, D)
    y = jnp.zeros_like(xt).at[slot_tok].add((ys * slot_w[:, None]).astype(x.dtype))
    return y.reshape(B, S, D)


def encoder_trunk(x, attn_norm, w_in, b_gate, q_norm, k_norm, w_proj_a, w_proj_b, w_out,
                  ffn_norm, ffn_w_gate, ffn_w_up, ffn_w_down,
                  moe_router, moe_w_gate, moe_w_up, moe_w_down, final_norm):
    for layer in range(DEPTH):
        x = gated_mixer_layer(x, attn_norm[layer], w_in[layer], b_gate[layer], q_norm[layer],
                              k_norm[layer], w_proj_a[layer], w_proj_b[layer], w_out[layer])
        xn = rms_norm(x, ffn_norm[layer])
        j = layer // 2
        if layer % 2 == 0:
            x = x + swiglu(xn, ffn_w_gate[j], ffn_w_up[j], ffn_w_down[j])
        else:
            x = x + moe_swiglu(xn, moe_router[j], moe_w_gate[j], moe_w_up[j], moe_w_down[j])
    return rms_norm(x, final_norm)


def setup_inputs(seed: int = 0) -> dict:
    key = jax.random.key(seed)
    ks = jax.random.split(key, 20)
    f32 = jnp.float32

    def nrm(k, shape, fan_in):
        return jax.random.normal(k, shape, f32) * (fan_in ** -0.5)

    def gain(k, shape):
        return 1.0 + 0.02 * jax.random.normal(k, shape, f32)

    return {
        'x_prompt': jax.random.normal(ks[0], (BATCH, SEQ, D_MODEL), f32),
        'x_sample': jax.random.normal(ks[1], (DEC_BATCH, DEC_SEQ, D_MODEL), f32),
        'attn_norm': gain(ks[2], (DEPTH, D_MODEL)),
        'w_in': nrm(ks[3], (DEPTH, D_MODEL, C_IN), D_MODEL),
        'b_gate': 0.02 * jax.random.normal(ks[4], (DEPTH, C_GATE), f32),
        'q_norm': gain(ks[5], (DEPTH, HEAD_DIM)),
        'k_norm': gain(ks[6], (DEPTH, HEAD_DIM)),
        'w_proj_a': nrm(ks[7], (DEPTH, A_WIDTH, D_MODEL), A_WIDTH),
        'w_proj_b': nrm(ks[8], (DEPTH, B_WIDTH, D_MODEL), B_WIDTH),
        'w_out': nrm(ks[9], (DEPTH, D_MODEL, D_MODEL), D_MODEL),
        'ffn_norm': gain(ks[10], (DEPTH, D_MODEL)),
        'ffn_w_gate': nrm(ks[11], (N_DENSE, D_MODEL, D_FF), D_MODEL),
        'ffn_w_up': nrm(ks[12], (N_DENSE, D_MODEL, D_FF), D_MODEL),
        'ffn_w_down': nrm(ks[13], (N_DENSE, D_FF, D_MODEL), D_FF),
        'moe_router': nrm(ks[14], (N_MOE, D_MODEL, N_EXPERTS), D_MODEL),
        'moe_w_gate': nrm(ks[15], (N_MOE, N_EXPERTS, D_MODEL, D_FF_EXPERT), D_MODEL),
        'moe_w_up': nrm(ks[16], (N_MOE, N_EXPERTS, D_MODEL, D_FF_EXPERT), D_MODEL),
        'moe_w_down': nrm(ks[17], (N_MOE, N_EXPERTS, D_FF_EXPERT, D_MODEL), D_FF_EXPERT),
        'final_norm': gain(ks[18], (D_MODEL,)),
    }


def reference(x_prompt, x_sample, attn_norm, w_in, b_gate, q_norm, k_norm, w_proj_a, w_proj_b,
              w_out, ffn_norm, ffn_w_gate, ffn_w_up, ffn_w_down, moe_router, moe_w_gate,
              moe_w_up, moe_w_down, final_norm):
    y_prompt = encoder_trunk(x_prompt, attn_norm, w_in, b_gate, q_norm, k_norm, w_proj_a,
                             w_proj_b, w_out, ffn_norm, ffn_w_gate, ffn_w_up, ffn_w_down,
                             moe_router, moe_w_gate, moe_w_up, moe_w_down, final_norm)
    y_sample = encoder_trunk(x_sample, attn_norm, w_in, b_gate, q_norm, k_norm, w_proj_a,
                             w_proj_b, w_out, ffn_norm, ffn_w_gate, ffn_w_up, ffn_w_down,
                             moe_router, moe_w_gate, moe_w_up, moe_w_down, final_norm)
    return (y_prompt, y_sample)
```

```python
import functools
import math

import jax
import jax.numpy as jnp
from jax import lax
from jax.experimental import pallas as pl
from jax.experimental.pallas import tpu as pltpu

F32 = jnp.float32
BF16 = jnp.bfloat16

D_MODEL = 2048
HEAD_DIM = 128
ATTN_SCALE = HEAD_DIM ** -0.5
EPS = 1e-6
NEG = -1e30

A_WINDOWS = (128, 512, 2048)
A_DILATIONS = (1, 4, 16)
A_N_GROUPS = 3
A_HEADS = 4
A_WIDTH = A_HEADS * HEAD_DIM
A_QK = 2 * A_N_GROUPS * A_WIDTH
A_QKV = 3 * A_N_GROUPS * A_WIDTH
A_RADIUS = 64
ROPE_THETA = 500000.0
ROPE_DIMS = HEAD_DIM // 4

B_Q_HEADS = 8
B_KV_HEADS = 2
B_GROUP = B_Q_HEADS // B_KV_HEADS
B_WIDTH = B_Q_HEADS * HEAD_DIM
B_KVW = B_KV_HEADS * HEAD_DIM
B_QKV = B_WIDTH + 2 * B_KVW
AXIAL_THETA = 10000.0
GRID_W = 64

C_GATE = 2 * D_MODEL
N_EXPERTS = 8
TOP_K = 2

V7X_VMEM_BYTES = 64 * 1024 * 1024
VMEM_LIMIT = V7X_VMEM_BYTES - 8 * 1024 * 1024
LANES = 128

LOG2E = math.log2(math.e)


def _cparams(n_axes, vmem=VMEM_LIMIT):
    return pltpu.CompilerParams(dimension_semantics=("arbitrary",) * n_axes,
                                vmem_limit_bytes=vmem)


def _rms(x, g):
    return x * lax.rsqrt(jnp.mean(x * x, axis=-1, keepdims=True) + EPS) * g


def _rope128(x, c, s_lo, s_hi, w):
    return x * c + pltpu.roll(x, LANES - w, 1) * s_lo + pltpu.roll(x, w, 1) * s_hi


def _norm_in_kernel(xp_ref, xs_ref, g_ref, x_ref, xn_ref, *, n_p):
    i = pl.program_id(0)

    def emit(src):
        x = src[...]
        x_ref[...] = x
        xn_ref[...] = _rms(x, g_ref[...]).astype(BF16)

    @pl.when(i < n_p)
    def _():
        emit(xp_ref)

    @pl.when(i >= n_p)
    def _():
        emit(xs_ref)


def _norm_in(xp, xs, g, bm):
    tp, ts = xp.shape[0], xs.shape[0]
    n_p, n_s = tp // bm, ts // bm
    t = tp + ts
    return pl.pallas_call(
        functools.partial(_norm_in_kernel, n_p=n_p),
        grid=(n_p + n_s,),
        in_specs=[pl.BlockSpec((bm, D_MODEL), lambda i: (jnp.minimum(i, n_p - 1), 0)),
                  pl.BlockSpec((bm, D_MODEL), lambda i: (jnp.maximum(i - n_p, 0), 0)),
                  pl.BlockSpec((1, D_MODEL), lambda i: (0, 0))],
        out_specs=[pl.BlockSpec((bm, D_MODEL), lambda i: (i, 0)),
                   pl.BlockSpec((bm, D_MODEL), lambda i: (i, 0))],
        out_shape=[jax.ShapeDtypeStruct((t, D_MODEL), F32),
                   jax.ShapeDtypeStruct((t, D_MODEL), BF16)],
        compiler_params=_cparams(1),
        name="norm_in",
    )(xp, xs, g)


def _mm_kernel(a_ref, w_ref, *refs, epilogue, n_extra):
    acc = jnp.dot(a_ref[...], w_ref[...], preferred_element_type=F32)
    epilogue(acc, refs[:n_extra], refs[n_extra:])


def _matmul(a, w, *, bm, bn, epilogue, extras=(), extra_specs=(), out_shape, out_specs, name):
    m, k = a.shape
    n = w.shape[1]
    assert m % bm == 0 and n % bn == 0, (m, n, bm, bn)
    return pl.pallas_call(
        functools.partial(_mm_kernel, epilogue=epilogue, n_extra=len(extras)),
        grid=(m // bm, n // bn),
        in_specs=[pl.BlockSpec((bm, k), lambda i, j: (i, 0)),
                  pl.BlockSpec((k, bn), lambda i, j: (0, j)),
                  *extra_specs],
        out_specs=out_specs,
        out_shape=out_shape,
        compiler_params=_cparams(2),
        name=name,
    )(a, w, *extras)


def _epi_plain(acc, extras, outs):
    outs[0][...] = acc.astype(outs[0].dtype)


def _epi_gate(acc, extras, outs):
    outs[0][...] = jax.nn.sigmoid(acc + extras[0][...]).astype(outs[0].dtype)


def _epi_rope_a(acc, extras, outs):
    c, s_lo, s_hi = (r[...] for r in extras)
    w = ROPE_DIMS // 2
    for h in range(acc.shape[1] // LANES):
        sl = slice(h * LANES, (h + 1) * LANES)
        outs[0][:, sl] = _rope128(acc[:, sl], c, s_lo, s_hi, w).astype(outs[0].dtype)


def _epi_norm_rope_b(acc, extras, outs, *, scale):
    g = extras[0][...]
    c, s_lo, s_hi = (r[...] for r in extras[1:])
    w = HEAD_DIM // 4
    for h in range(acc.shape[1] // LANES):
        sl = slice(h * LANES, (h + 1) * LANES)
        y = _rope128(_rms(acc[:, sl], g), c, s_lo, s_hi, w)
        if scale != 1.0:
            y = y * scale
        outs[0][:, sl] = y.astype(outs[0].dtype)


def _epi_residual_norm(acc, extras, outs):
    x = extras[0][...] + acc
    outs[0][...] = x
    outs[1][...] = _rms(x, extras[1][...]).astype(outs[1].dtype)


def _attn_a_tiles(l_sub):
    if l_sub <= 256 + 2 * A_RADIUS:
        if l_sub >= 128 + 2 * A_RADIUS and l_sub % 128 == 0:
            return 128, 128 + 2 * A_RADIUS
        return l_sub, l_sub
    return 256, 256 + 2 * A_RADIUS


def _attn_a_kernel(q_ref, k_ref, v_ref, o_ref, lse_ref, *, d, n_short_blk, sp, ss):
    def rows(start, size):
        if d == 1:
            return pl.ds(start, size)
        return pl.ds(start, size, stride=d)

    def run(seq, n_seq):
        l_sub = seq // d
        tq, win = _attn_a_tiles(l_sub)
        n_tile = l_sub // tq

        def tile(idx, carry):
            base = (idx // (d * n_tile)) * seq
            r = (idx // n_tile) % d
            t0 = (idx % n_tile) * tq
            ks = jnp.clip(t0 - A_RADIUS, 0, l_sub - win)
            q = q_ref[rows(base + r + t0 * d, tq), :].astype(BF16)
            k = k_ref[rows(base + r + ks * d, win), :].astype(BF16)
            v = v_ref[rows(base + r + ks * d, win), :].astype(BF16)
            s = lax.dot_general(q, k, (((1,), (1,)), ((), ())),
                                preferred_element_type=F32) * ATTN_SCALE
            qpos = t0 + lax.broadcasted_iota(jnp.int32, (tq, win), 0)
            kpos = ks + lax.broadcasted_iota(jnp.int32, (tq, win), 1)
            s = jnp.where(jnp.abs(kpos - qpos) <= A_RADIUS, s, NEG)
            m = jnp.max(s, axis=-1, keepdims=True)
            p = jnp.exp(s - m)
            den = jnp.sum(p, axis=-1, keepdims=True)
            o = jnp.dot(p.astype(BF16), v, preferred_element_type=F32) / den
            lse = m + jnp.log(den)
            o_ref[rows(base + r + t0 * d, tq), :] = o
            lse_ref[rows(base + r + t0 * d, tq), :] = jnp.broadcast_to(lse, (tq, LANES))
            return carry

        lax.fori_loop(0, n_seq * d * n_tile, tile, 0)

    if sp == ss:
        run(ss, 1)
    else:
        blk = pl.program_id(0)

        @pl.when(blk < n_short_blk)
        def _():
            run(sp, ss // sp)

        @pl.when(blk >= n_short_blk)
        def _():
            run(ss, 1)


def _attn_a(qk, v, g, tp, sp, ss):
    t = qk.shape[0]
    kern = functools.partial(_attn_a_kernel, d=A_DILATIONS[g], n_short_blk=tp // ss, sp=sp, ss=ss)
    return pl.pallas_call(
        kern,
        grid=(t // ss, A_HEADS),
        in_specs=[pl.BlockSpec((ss, LANES), lambda b, h: (b, g * A_HEADS + h)),
                  pl.BlockSpec((ss, LANES), lambda b, h: (b, (A_N_GROUPS + g) * A_HEADS + h)),
                  pl.BlockSpec((ss, LANES), lambda b, h: (b, g * A_HEADS + h))],
        out_specs=[pl.BlockSpec((ss, LANES), lambda b, h: (b, h))] * 2,
        out_shape=[jax.ShapeDtypeStruct((t, A_WIDTH), F32)] * 2,
        compiler_params=_cparams(2),
        name=f"attn_a{g}",
    )(qk, qk, v)


def _attn_b_kernel(q_ref, k_ref, v_ref, o_ref, *, tq, tk, tp, sp, ss):
    row = pl.program_id(1) * tq
    short = row < tp
    first = jnp.where(short, ((row % ss) // sp) * sp, 0)
    n_chunk = jnp.where(short, sp // tk, ss // tk)
    q = jnp.concatenate([q_ref[:, h * LANES:(h + 1) * LANES] for h in range(B_GROUP)], axis=0)
    rows = B_GROUP * tq

    def chunk(c, carry):
        m, l, acc = carry
        start = pl.multiple_of(first + c * tk, tk)
        k = k_ref[pl.ds(start, tk), :]
        v = v_ref[pl.ds(start, tk), :]
        s = lax.dot_general(q, k, (((1,), (1,)), ((), ())), preferred_element_type=F32)
        m_new = jnp.maximum(m, jnp.max(s, axis=-1, keepdims=True))
        alpha = jnp.exp2(m - m_new)
        p = jnp.exp2(s - m_new)
        l = alpha * l + jnp.sum(p, axis=-1, keepdims=True)
        acc = alpha * acc + jnp.dot(p.astype(BF16), v, preferred_element_type=F32)
        return m_new, l, acc

    m0 = jnp.full((rows, 1), NEG, F32)
    l0 = jnp.zeros((rows, 1), F32)
    a0 = jnp.zeros((rows, HEAD_DIM), F32)
    _, l, acc = lax.fori_loop(0, n_chunk, chunk, (m0, l0, a0))
    o = acc / l
    for h in range(B_GROUP):
        o_ref[:, h * LANES:(h + 1) * LANES] = o[h * tq:(h + 1) * tq].astype(o_ref.dtype)


def _attn_b(q, k, v, tp, sp, ss, tq, tk):
    t = q.shape[0]
    gw = B_GROUP * HEAD_DIM
    per_blk = ss // tq
    return pl.pallas_call(
        functools.partial(_attn_b_kernel, tq=tq, tk=tk, tp=tp, sp=sp, ss=ss),
        grid=(B_KV_HEADS, t // tq),
        in_specs=[pl.BlockSpec((tq, gw), lambda kv, i: (i, kv)),
                  pl.BlockSpec((ss, HEAD_DIM), lambda kv, i: (i // per_blk, kv)),
                  pl.BlockSpec((ss, HEAD_DIM), lambda kv, i: (i // per_blk, kv))],
        out_specs=pl.BlockSpec((tq, gw), lambda kv, i: (i, kv)),
        out_shape=jax.ShapeDtypeStruct((t, B_WIDTH), BF16),
        compiler_params=_cparams(2),
        name="attn_b",
    )(q, k, v)


def _merge_kernel(o0, o1, o2, l0, l1, l2, ob_ref, gate_ref, wpa_ref, wpb_ref, out_ref):
    lse = [l0[...], l1[...], l2[...]]
    top = jnp.maximum(jnp.maximum(lse[0], lse[1]), lse[2])
    e = [jnp.exp(x - top) for x in lse]
    oa = (e[0] * o0[...] + e[1] * o1[...] + e[2] * o2[...]) / (e[0] + e[1] + e[2])
    pa = jnp.dot(oa.astype(BF16), wpa_ref[...], preferred_element_type=F32)
    pb = jnp.dot(ob_ref[...], wpb_ref[...], preferred_element_type=F32)
    ga = gate_ref[:, :D_MODEL].astype(F32)
    gb = gate_ref[:, D_MODEL:].astype(F32)
    out_ref[...] = (ga * pa + gb * pb).astype(out_ref.dtype)


def _merge(oa, lse, ob, gates, wpa, wpb, bm):
    t = ob.shape[0]
    row = lambda w: pl.BlockSpec((bm, w), lambda i: (i, 0))
    const = lambda s: pl.BlockSpec(s, lambda i: (0, 0))
    return pl.pallas_call(
        _merge_kernel,
        grid=(t // bm,),
        in_specs=[row(A_WIDTH)] * 6 + [row(B_WIDTH), row(C_GATE),
                                       const((A_WIDTH, D_MODEL)), const((B_WIDTH, D_MODEL))],
        out_specs=row(D_MODEL),
        out_shape=jax.ShapeDtypeStruct((t, D_MODEL), BF16),
        compiler_params=_cparams(1),
        name="merge",
    )(*oa, *lse, ob, gates, wpa, wpb)


def _ffn_kernel(xn_ref, wg_ref, wu_ref, wd_ref, x_ref, gn_ref, o_ref, on_ref, *, n_chunk):
    c = pl.program_id(1)

    @pl.when(c == 0)
    def _():
        o_ref[...] = x_ref[...]

    xn = xn_ref[...]
    g = jnp.dot(xn, wg_ref[...], preferred_element_type=F32)
    u = jnp.dot(xn, wu_ref[...], preferred_element_type=F32)
    hm = (g * jax.nn.sigmoid(g) * u).astype(BF16)
    o_ref[...] += jnp.dot(hm, wd_ref[...], preferred_element_type=F32)

    @pl.when(c == n_chunk - 1)
    def _():
        on_ref[...] = _rms(o_ref[...], gn_ref[...]).astype(on_ref.dtype)


def _ffn(xn, wg, wu, wd, x, g_next, bm, bc):
    t = xn.shape[0]
    f = wg.shape[1]
    n_chunk = f // bc
    row = lambda: pl.BlockSpec((bm, D_MODEL), lambda i, c: (i, 0))
    return pl.pallas_call(
        functools.partial(_ffn_kernel, n_chunk=n_chunk),
        grid=(t // bm, n_chunk),
        in_specs=[row(),
                  pl.BlockSpec((D_MODEL, bc), lambda i, c: (0, c)),
                  pl.BlockSpec((D_MODEL, bc), lambda i, c: (0, c)),
                  pl.BlockSpec((bc, D_MODEL), lambda i, c: (c, 0)),
                  row(),
                  pl.BlockSpec((1, D_MODEL), lambda i, c: (0, 0))],
        out_specs=[row(), row()],
        out_shape=[jax.ShapeDtypeStruct((t, D_MODEL), F32),
                   jax.ShapeDtypeStruct((t, D_MODEL), BF16)],
        compiler_params=_cparams(2),
        name="ffn_dense",
    )(xn, wg, wu, wd, x, g_next)


def _router_kernel(x_ref, g_ref, wr_ref, xn_ref, route_ref):
    xn = _rms(x_ref[...], g_ref[...])
    xn_ref[...] = xn
    logits = jnp.dot(xn, wr_ref[...], preferred_element_type=F32,
                     precision=lax.Precision.HIGHEST)
    lane = lax.broadcasted_iota(jnp.int32, logits.shape, 1)
    lg = jnp.where(lane < N_EXPERTS, logits, -jnp.inf)
    v1 = jnp.max(lg, axis=-1, keepdims=True)
    i1 = jnp.min(jnp.where(lg == v1, lane, LANES), axis=-1, keepdims=True)
    lg2 = jnp.where(lane == i1, -jnp.inf, lg)
    v2 = jnp.max(lg2, axis=-1, keepdims=True)
    i2 = jnp.min(jnp.where(lg2 == v2, lane, LANES), axis=-1, keepdims=True)
    e2 = jnp.exp(v2 - v1)
    w1 = 1.0 / (1.0 + e2)
    w2 = e2 / (1.0 + e2)
    route = jnp.where(lane == 0, i1.astype(F32),
                      jnp.where(lane == 1, i2.astype(F32),
                                jnp.where(lane == 2, w1, jnp.where(lane == 3, w2, 0.0))))
    route_ref[...] = route


def _router(x, g, wr_pad, bm):
    t = x.shape[0]
    return pl.pallas_call(
        _router_kernel,
        grid=(t // bm,),
        in_specs=[pl.BlockSpec((bm, D_MODEL), lambda i: (i, 0)),
                  pl.BlockSpec((1, D_MODEL), lambda i: (0, 0)),
                  pl.BlockSpec((D_MODEL, LANES), lambda i: (0, 0))],
        out_specs=[pl.BlockSpec((bm, D_MODEL), lambda i: (i, 0)),
                   pl.BlockSpec((bm, LANES), lambda i: (i, 0))],
        out_shape=[jax.ShapeDtypeStruct((t, D_MODEL), F32),
                   jax.ShapeDtypeStruct((t, LANES), F32)],
        compiler_params=_cparams(1),
        name="router",
    )(x, g, wr_pad)


def _moe_kernel(be_ref, nvalid_ref,
                tok_ref, dest_ref, xn_hbm, wg_ref, wu_ref, wd_ref, wb_ref, y_hbm,
                xs_f32, xs_bf, acc, sem_in, sem_out, *, bm, n_chunk):
    i = pl.program_id(0)
    c = pl.program_id(1)
    n_valid = nvalid_ref[i]
    used = n_valid > 0

    def row_in(r, t):
        return pltpu.make_async_copy(xn_hbm.at[pl.ds(t, 1), :], xs_f32.at[pl.ds(r, 1), :], sem_in)

    def row_out(r, t):
        return pltpu.make_async_copy(acc.at[pl.ds(r, 1), :], y_hbm.at[pl.ds(t, 1), :], sem_out)

    @pl.when(jnp.logical_and(used, c == 0))
    def _():
        def start(r, carry):
            row_in(r, tok_ref[0, r]).start()
            return carry

        def wait(r, carry):
            row_in(r, 0).wait()
            return carry

        lax.fori_loop(0, bm, start, 0)
        lax.fori_loop(0, bm, wait, 0)
        xs_bf[...] = xs_f32[...].astype(BF16)

    @pl.when(used)
    def _():
        xs = xs_bf[...]
        g = jnp.dot(xs, wg_ref[...], preferred_element_type=F32)
        u = jnp.dot(xs, wu_ref[...], preferred_element_type=F32)
        hm = (g * jax.nn.sigmoid(g) * u).astype(BF16)
        part = jnp.dot(hm, wd_ref[...], preferred_element_type=F32)

        @pl.when(c == 0)
        def _():
            acc[...] = part

        @pl.when(c > 0)
        def _():
            acc[...] += part

    @pl.when(jnp.logical_and(used, c == n_chunk - 1))
    def _():
        acc[...] = acc[...] * jnp.tile(wb_ref[...], (1, D_MODEL // LANES))

        def start(r, carry):
            row_out(r, dest_ref[0, r]).start()
            return carry

        def wait(r, carry):
            row_out(r, 0).wait()
            return carry

        lax.fori_loop(0, n_valid, start, 0)
        lax.fori_loop(0, n_valid, wait, 0)


def _moe(xn, wg, wu, wd, blk_expert, slot_tok, slot_dest, blk_valid, slot_wb, n_rows_out, bm, bc):
    n_slots = slot_tok.shape[0]
    n_blocks = n_slots // bm
    f = wg.shape[2]
    n_chunk = f // bc
    idx_spec = pl.BlockSpec((None, 1, bm), lambda i, c, *_: (i, 0, 0), memory_space=pltpu.SMEM)
    grid_spec = pltpu.PrefetchScalarGridSpec(
        num_scalar_prefetch=2,
        grid=(n_blocks, n_chunk),
        in_specs=[idx_spec, idx_spec,
                  pl.BlockSpec(memory_space=pl.ANY),
                  pl.BlockSpec((None, D_MODEL, bc), lambda i, c, be, *_: (be[i], 0, c)),
                  pl.BlockSpec((None, D_MODEL, bc), lambda i, c, be, *_: (be[i], 0, c)),
                  pl.BlockSpec((None, bc, D_MODEL), lambda i, c, be, *_: (be[i], c, 0)),
                  pl.BlockSpec((bm, LANES), lambda i, c, *_: (i, 0))],
        out_specs=pl.BlockSpec(memory_space=pl.ANY),
        scratch_shapes=[pltpu.VMEM((bm, D_MODEL), F32),
                        pltpu.VMEM((bm, D_MODEL), BF16),
                        pltpu.VMEM((bm, D_MODEL), F32),
                        pltpu.SemaphoreType.DMA(()),
                        pltpu.SemaphoreType.DMA(())],
    )
    return pl.pallas_call(
        functools.partial(_moe_kernel, bm=bm, n_chunk=n_chunk),
        grid_spec=grid_spec,
        out_shape=jax.ShapeDtypeStruct((n_rows_out, D_MODEL), F32),
        compiler_params=_cparams(2),
        name="moe_experts",
    )(blk_expert, blk_valid, slot_tok.reshape(n_blocks, 1, bm), slot_dest.reshape(n_blocks, 1, bm),
      xn, wg, wu, wd, slot_wb)


def _final_kernel(x_ref, y0_ref, y1_ref, g_ref, o_ref):
    x = x_ref[...] + (y0_ref[...] + y1_ref[...])
    o_ref[...] = _rms(x, g_ref[...])


def _final(x, y, g, row0, rows, t, bm):
    rb, tb = row0 // bm, t // bm
    return pl.pallas_call(
        _final_kernel,
        grid=(rows // bm,),
        in_specs=[pl.BlockSpec((bm, D_MODEL), lambda i: (rb + i, 0)),
                  pl.BlockSpec((bm, D_MODEL), lambda i: (rb + i, 0)),
                  pl.BlockSpec((bm, D_MODEL), lambda i: (tb + rb + i, 0)),
                  pl.BlockSpec((1, D_MODEL), lambda i: (0, 0))],
        out_specs=pl.BlockSpec((bm, D_MODEL), lambda i: (i, 0)),
        out_shape=jax.ShapeDtypeStruct((rows, D_MODEL), F32),
        compiler_params=_cparams(1),
        name="final_norm",
    )(x, y, y, g)


def _rope_tables(n_pos):
    pos = jnp.arange(n_pos, dtype=F32)
    ones = jnp.ones((n_pos, 1), F32)
    zeros = jnp.zeros((n_pos, 1), F32)

    def angles(p, dims, theta):
        inv = jnp.power(jnp.float32(theta), -jnp.arange(0, dims, 2, dtype=F32) / dims)
        return p[:, None] * inv[None, :]

    ang = angles(pos, ROPE_DIMS, ROPE_THETA)
    rest = HEAD_DIM - ROPE_DIMS
    cos_a = jnp.concatenate([jnp.cos(ang), jnp.cos(ang), jnp.tile(ones, (1, rest))], axis=1)
    half = jnp.tile(zeros, (1, ROPE_DIMS // 2))
    tail = jnp.tile(zeros, (1, rest))
    lo_a = jnp.concatenate([-jnp.sin(ang), half, tail], axis=1)
    hi_a = jnp.concatenate([half, jnp.sin(ang), tail], axis=1)
    row = jnp.floor(pos / GRID_W)
    col = pos - row * GRID_W
    ar = angles(row, HEAD_DIM // 2, AXIAL_THETA)
    ac = angles(col, HEAD_DIM // 2, AXIAL_THETA)
    quarter = jnp.tile(zeros, (1, HEAD_DIM // 4))
    cos_b = jnp.concatenate([jnp.cos(ar), jnp.cos(ar), jnp.cos(ac), jnp.cos(ac)], axis=1)
    lo_b = jnp.concatenate([-jnp.sin(ar), quarter, -jnp.sin(ac), quarter], axis=1)
    hi_b = jnp.concatenate([quarter, jnp.sin(ar), quarter, jnp.sin(ac)], axis=1)
    return (cos_a, lo_a, hi_a), (cos_b, lo_b, hi_b)


def _routing_slots(route, t, bm):
    n_asg = t * TOP_K
    flat_e = route[:, :TOP_K].astype(jnp.int32).reshape(-1)
    flat_w = route[:, TOP_K:2 * TOP_K].reshape(-1)
    onehot = (flat_e[:, None] == jnp.arange(N_EXPERTS, dtype=jnp.int32)[None, :]).astype(jnp.int32)
    csum = jnp.cumsum(onehot, axis=0)
    counts = csum[-1]
    rank = jnp.take_along_axis(csum, flat_e[:, None], axis=1)[:, 0] - 1
    padded = (counts + bm - 1) // bm * bm
    padded_end = jnp.cumsum(padded)
    padded_start = padded_end - padded
    slot = padded_start[flat_e] + rank
    n_slots = n_asg + N_EXPERTS * bm
    n_blocks = n_slots // bm
    asg = jnp.arange(n_asg, dtype=jnp.int32)
    tok = asg // TOP_K
    slot_tok = jnp.zeros((n_slots,), jnp.int32).at[slot].set(tok)
    slot_dest = jnp.zeros((n_slots,), jnp.int32).at[slot].set((asg % TOP_K) * t + tok)
    slot_w = jnp.zeros((n_slots,), F32).at[slot].set(flat_w)
    blk_start = jnp.arange(n_blocks, dtype=jnp.int32) * bm
    blk_expert = jnp.minimum(jnp.searchsorted(padded_end, blk_start, side='right'),
                             N_EXPERTS - 1).astype(jnp.int32)
    blk_valid = jnp.clip(padded_start[blk_expert] + counts[blk_expert] - blk_start, 0, bm)
    blk_valid = jnp.where(blk_start < padded_end[-1], blk_valid, 0).astype(jnp.int32)
    slot_wb = jnp.broadcast_to(slot_w[:, None], (n_slots, LANES))
    return blk_expert, slot_tok, slot_dest, blk_valid, slot_wb


def _tile(n, pref):
    while n % pref:
        pref //= 2
    return pref


def _trunk(x_prompt, x_sample, attn_norm, w_in, b_gate, q_norm, k_norm, w_proj_a, w_proj_b,
           w_out, ffn_norm, ffn_w_gate, ffn_w_up, ffn_w_down, moe_router, moe_w_gate,
           moe_w_up, moe_w_down, final_norm):
    bp, sp, _ = x_prompt.shape
    bs, ss, _ = x_sample.shape
    tp, ts = bp * sp, bs * ss
    t = tp + ts
    depth = w_in.shape[0]
    assert depth == 2, "dense-FFN layer followed by a final MoE layer"
    assert ss % sp == 0 and tp % ss == 0, "blocks of ss tokens must hold whole sequences"
    sets = ((0, tp), (tp, ts))

    seq_unit = math.gcd(sp, ss)
    bm = _tile(seq_unit, 1024)
    bm_small = _tile(seq_unit, 512)
    n_p_blk, n_pb, n_sb = tp // bm, sp // bm, ss // bm

    def pos_blk(i, j):
        return (jnp.where(i < n_p_blk, i % n_pb, (i - n_p_blk) % n_sb), 0)

    tab_a, tab_b = _rope_tables(max(sp, ss))
    tab_specs = [pl.BlockSpec((bm, LANES), pos_blk)] * 3

    row2 = lambda v: v.reshape(1, -1).astype(F32)
    x, xn = _norm_in(x_prompt.reshape(tp, D_MODEL), x_sample.reshape(ts, D_MODEL),
                     row2(attn_norm[0]), bm_small)

    def proj(a, w, n_lo, n_hi, bn, epilogue, extras, extra_specs, dtype, name):
        n = n_hi - n_lo
        return _matmul(a, w[:, n_lo:n_hi].astype(BF16), bm=bm, bn=bn, epilogue=epilogue,
                       extras=extras, extra_specs=extra_specs,
                       out_shape=[jax.ShapeDtypeStruct((t, n), dtype)],
                       out_specs=[pl.BlockSpec((bm, bn), lambda i, j: (i, j))], name=name)[0]

    out = None
    for layer in range(depth):
        wi = w_in[layer]
        b0 = A_QKV
        qk_a = proj(xn, wi, 0, A_QK, 512, _epi_rope_a, tab_a, tab_specs, F32, "proj_a_qk")
        v_a = proj(xn, wi, A_QK, A_QKV, 512, _epi_plain, (), (), F32, "proj_a_v")
        gq = [row2(q_norm[layer]), *tab_b]
        gk = [row2(k_norm[layer]), *tab_b]
        g_spec = [pl.BlockSpec((1, LANES), lambda i, j: (0, 0)), *tab_specs]
        q_b = proj(xn, wi, b0, b0 + B_WIDTH, 512,
                   functools.partial(_epi_norm_rope_b, scale=ATTN_SCALE * LOG2E), gq, g_spec,
                   BF16, "proj_b_q")
        k_b = proj(xn, wi, b0 + B_WIDTH, b0 + B_WIDTH + B_KVW, B_KVW,
                   functools.partial(_epi_norm_rope_b, scale=1.0), gk, g_spec, BF16, "proj_b_k")
        v_b = proj(xn, wi, b0 + B_WIDTH + B_KVW, b0 + B_QKV, B_KVW, _epi_plain, (), (), BF16,
                   "proj_b_v")
        gates = proj(xn, wi, b0 + B_QKV, b0 + B_QKV + C_GATE, 512, _epi_gate,
                     [row2(b_gate[layer])], [pl.BlockSpec((1, 512), lambda i, j: (0, j))],
                     BF16, "proj_gates")

        oa, lse = [], []
        for g in range(A_N_GROUPS):
            o_g, lse_g = _attn_a(qk_a, v_a, g, tp, sp, ss)
            oa.append(o_g)
            lse.append(lse_g)
        ob = _attn_b(q_b, k_b, v_b, tp, sp, ss, _tile(seq_unit, 256), _tile(seq_unit, 512))

        merged = _merge(oa, lse, ob, gates, w_proj_a[layer].astype(BF16),
                        w_proj_b[layer].astype(BF16), bm_small // 2)
        x, xn_ffn = _matmul(
            merged, w_out[layer].astype(BF16), bm=bm_small, bn=D_MODEL,
            epilogue=_epi_residual_norm,
            extras=[x, row2(ffn_norm[layer])],
            extra_specs=[pl.BlockSpec((bm_small, D_MODEL), lambda i, j: (i, 0)),
                         pl.BlockSpec((1, D_MODEL), lambda i, j: (0, 0))],
            out_shape=[jax.ShapeDtypeStruct((t, D_MODEL), F32),
                       jax.ShapeDtypeStruct((t, D_MODEL), BF16)],
            out_specs=[pl.BlockSpec((bm_small, D_MODEL), lambda i, j: (i, 0))] * 2,
            name="proj_out")

        j = layer // 2
        if layer % 2 == 0:
            x, xn = _ffn(xn_ffn, ffn_w_gate[j].astype(BF16), ffn_w_up[j].astype(BF16),
                         ffn_w_down[j].astype(BF16), x, row2(attn_norm[layer + 1]), bm_small, 512)
        else:
            wr = jnp.pad(moe_router[j].astype(F32), ((0, 0), (0, LANES - N_EXPERTS)))
            xn_moe, route = _router(x, row2(ffn_norm[layer]), wr, bm_small)
            meta = _routing_slots(route, t, bm_small)
            n_rows_out = TOP_K * t
            y = _moe(xn_moe, moe_w_gate[j].astype(BF16), moe_w_up[j].astype(BF16),
                     moe_w_down[j].astype(BF16), *meta, n_rows_out, bm_small, 512)
            out = tuple(_final(x, y, row2(final_norm), row0, rows, t, bm_small)
                        for row0, rows in sets)
    y_prompt, y_sample = out
    return y_prompt.reshape(bp, sp, D_MODEL), y_sample.reshape(bs, ss, D_MODEL)


def kernel(x_prompt, x_sample, attn_norm, w_in, b_gate, q_norm, k_norm, w_proj_a, w_proj_b,
           w_out, ffn_norm, ffn_w_gate, ffn_w_up, ffn_w_down, moe_router, moe_w_gate,
           moe_w_up, moe_w_down, final_norm):
    return _trunk(x_prompt, x_sample, attn_norm, w_in, b_gate, q_norm, k_norm, w_proj_a,
                  w_proj_b, w_out, ffn_norm, ffn_w_gate, ffn_w_up, ffn_w_down, moe_router,
                  moe_w_gate, moe_w_up, moe_w_down, final_norm)
```

```python
import functools
import math

import jax
import jax.numpy as jnp
from jax import lax
from jax.experimental import pallas as pl
from jax.experimental.pallas import tpu as pltpu

F32 = jnp.float32
BF16 = jnp.bfloat16

D_MODEL = 2048
HEAD_DIM = 128
ATTN_SCALE = HEAD_DIM ** -0.5
EPS = 1e-6
NEG = -1e30

A_WINDOWS = (128, 512, 2048)
A_DILATIONS = (1, 4, 16)
A_N_GROUPS = 3
A_HEADS = 4
A_WIDTH = A_HEADS * HEAD_DIM
A_QK = 2 * A_N_GROUPS * A_WIDTH
A_QKV = 3 * A_N_GROUPS * A_WIDTH
A_RADIUS = 64
ROPE_THETA = 500000.0
ROPE_DIMS = HEAD_DIM // 4

B_Q_HEADS = 8
B_KV_HEADS = 2
B_GROUP = B_Q_HEADS // B_KV_HEADS
B_WIDTH = B_Q_HEADS * HEAD_DIM
B_KVW = B_KV_HEADS * HEAD_DIM
B_QKV = B_WIDTH + 2 * B_KVW
AXIAL_THETA = 10000.0
GRID_W = 64

C_GATE = 2 * D_MODEL
N_EXPERTS = 8
TOP_K = 2

V7X_VMEM_BYTES = 64 * 1024 * 1024
VMEM_LIMIT = V7X_VMEM_BYTES - 8 * 1024 * 1024
LANES = 128

LOG2E = math.log2(math.e)


def _cparams(n_axes, vmem=VMEM_LIMIT):
    return pltpu.CompilerParams(dimension_semantics=("arbitrary",) * n_axes,
                                vmem_limit_bytes=vmem)


def _rms(x, g):
    return x * lax.rsqrt(jnp.mean(x * x, axis=-1, keepdims=True) + EPS) * g


def _rope128(x, c, s_lo, s_hi, w):
    return x * c + pltpu.roll(x, LANES - w, 1) * s_lo + pltpu.roll(x, w, 1) * s_hi


def _norm_in_kernel(xp_ref, xs_ref, g_ref, x_ref, xn_ref, *, n_p):
    i = pl.program_id(0)

    def emit(src):
        x = src[...]
        x_ref[...] = x
        xn_ref[...] = _rms(x, g_ref[...]).astype(BF16)

    @pl.when(i < n_p)
    def _():
        emit(xp_ref)

    @pl.when(i >= n_p)
    def _():
        emit(xs_ref)


def _norm_in(xp, xs, g, bm):
    tp, ts = xp.shape[0], xs.shape[0]
    n_p, n_s = tp // bm, ts // bm
    t = tp + ts
    return pl.pallas_call(
        functools.partial(_norm_in_kernel, n_p=n_p),
        grid=(n_p + n_s,),
        in_specs=[pl.BlockSpec((bm, D_MODEL), lambda i: (jnp.minimum(i, n_p - 1), 0)),
                  pl.BlockSpec((bm, D_MODEL), lambda i: (jnp.maximum(i - n_p, 0), 0)),
                  pl.BlockSpec((1, D_MODEL), lambda i: (0, 0))],
        out_specs=[pl.BlockSpec((bm, D_MODEL), lambda i: (i, 0)),
                   pl.BlockSpec((bm, D_MODEL), lambda i: (i, 0))],
        out_shape=[jax.ShapeDtypeStruct((t, D_MODEL), F32),
                   jax.ShapeDtypeStruct((t, D_MODEL), BF16)],
        compiler_params=_cparams(1),
        name="norm_in",
    )(xp, xs, g)


def _mm_kernel(a_ref, w_ref, *refs, epilogue, n_extra):
    acc = jnp.dot(a_ref[...], w_ref[...], preferred_element_type=F32)
    epilogue(acc, refs[:n_extra], refs[n_extra:])


def _matmul(a, w, *, bm, bn, epilogue, extras=(), extra_specs=(), out_shape, out_specs, name):
    m, k = a.shape
    n = w.shape[1]
    assert m % bm == 0 and n % bn == 0, (m, n, bm, bn)
    return pl.pallas_call(
        functools.partial(_mm_kernel, epilogue=epilogue, n_extra=len(extras)),
        grid=(m // bm, n // bn),
        in_specs=[pl.BlockSpec((bm, k), lambda i, j: (i, 0)),
                  pl.BlockSpec((k, bn), lambda i, j: (0, j)),
                  *extra_specs],
        out_specs=out_specs,
        out_shape=out_shape,
        compiler_params=_cparams(2),
        name=name,
    )(a, w, *extras)


def _epi_residual_norm(acc, extras, outs):
    x = extras[0][...] + acc
    outs[0][...] = x
    outs[1][...] = _rms(x, extras[1][...]).astype(outs[1].dtype)


PROJ_BN = 2 * B_KVW
_J_VA = A_QK // PROJ_BN
_J_QB = A_QKV // PROJ_BN
_J_KV = (A_QKV + B_WIDTH) // PROJ_BN
_J_GATE = (A_QKV + B_QKV) // PROJ_BN
_J_END = (A_QKV + B_QKV + C_GATE) // PROJ_BN


def _proj_in_kernel(xn_ref, w_ref, ca_ref, la_ref, ha_ref, cb_ref, lb_ref, hb_ref,
                    gq_ref, gk_ref, bias_ref,
                    qka_ref, va_ref, qb_ref, kb_ref, vb_ref, gate_ref):
    j = pl.program_id(1)
    acc = jnp.dot(xn_ref[...], w_ref[...], preferred_element_type=F32)

    def heads(n):
        return [slice(h * LANES, (h + 1) * LANES) for h in range(n // LANES)]

    def norm_rope_b(x, g_ref):
        return _rope128(_rms(x, g_ref[...]), cb_ref[...], lb_ref[...], hb_ref[...], HEAD_DIM // 4)

    @pl.when(j < _J_VA)
    def _():
        for sl in heads(PROJ_BN):
            qka_ref[:, sl] = _rope128(acc[:, sl], ca_ref[...], la_ref[...], ha_ref[...],
                                      ROPE_DIMS // 2)

    @pl.when(jnp.logical_and(j >= _J_VA, j < _J_QB))
    def _():
        va_ref[...] = acc

    @pl.when(jnp.logical_and(j >= _J_QB, j < _J_KV))
    def _():
        tq = qb_ref.shape[2]
        for h, sl in enumerate(heads(PROJ_BN)):
            yt = (norm_rope_b(acc[:, sl], gq_ref) * (ATTN_SCALE * LOG2E)).T
            for n in range(qb_ref.shape[0]):
                qb_ref[n, h * HEAD_DIM:(h + 1) * HEAD_DIM, :] = (
                    yt[:, n * tq:(n + 1) * tq].astype(BF16))

    @pl.when(j == _J_KV)
    def _():
        for sl in heads(B_KVW):
            kb_ref[:, sl] = norm_rope_b(acc[:, sl], gk_ref).astype(BF16)
        tk = vb_ref.shape[2]
        vt = acc[:, B_KVW:].T
        for n in range(vb_ref.shape[0]):
            vb_ref[n] = vt[:, n * tk:(n + 1) * tk].astype(BF16)

    @pl.when(j >= _J_GATE)
    def _():
        gate_ref[...] = jax.nn.sigmoid(acc + bias_ref[...]).astype(BF16)


def _proj_in(xn, w, tab_a, tab_b, gq, gk, bias, pos_blk, bm):
    t = xn.shape[0]
    bn = PROJ_BN
    tq, tk = ATTN_B_TQ, ATTN_B_TK
    assert bm % tq == 0 and bm % tk == 0

    def cols(lo, hi):
        return lambda i, j: (i, jnp.clip(j - lo, 0, hi - lo - 1))

    tab_spec = pl.BlockSpec((bm, LANES), pos_blk)
    vec_spec = pl.BlockSpec((1, LANES), lambda i, j: (0, 0))
    return pl.pallas_call(
        _proj_in_kernel,
        grid=(t // bm, _J_END),
        in_specs=[pl.BlockSpec((bm, D_MODEL), lambda i, j: (i, 0)),
                  pl.BlockSpec((D_MODEL, bn), lambda i, j: (0, j)),
                  *([tab_spec] * 6), vec_spec, vec_spec,
                  pl.BlockSpec((1, bn), lambda i, j: (0, jnp.clip(j - _J_GATE, 0, _J_END - _J_GATE - 1)))],
        out_specs=[pl.BlockSpec((bm, bn), cols(0, _J_VA)),
                   pl.BlockSpec((bm, bn), cols(_J_VA, _J_QB)),
                   pl.BlockSpec((bm // tq, bn, tq),
                                lambda i, j: (i, jnp.clip(j - _J_QB, 0, _J_KV - _J_QB - 1), 0)),
                   pl.BlockSpec((bm, B_KVW), lambda i, j: (i, 0)),
                   pl.BlockSpec((bm // tk, B_KVW, tk), lambda i, j: (i, 0, 0)),
                   pl.BlockSpec((bm, bn), cols(_J_GATE, _J_END))],
        out_shape=[jax.ShapeDtypeStruct((t, A_QK), F32),
                   jax.ShapeDtypeStruct((t, A_QKV - A_QK), F32),
                   jax.ShapeDtypeStruct((t // tq, B_WIDTH, tq), BF16),
                   jax.ShapeDtypeStruct((t, B_KVW), BF16),
                   jax.ShapeDtypeStruct((t // tk, B_KVW, tk), BF16),
                   jax.ShapeDtypeStruct((t, C_GATE), BF16)],
        compiler_params=_cparams(2),
        name="proj_in",
    )(xn, w, *tab_a, *tab_b, gq, gk, bias)


ATTN_A_UNROLL = 4


def _attn_a_tiles(l_sub):
    if l_sub <= 256 + 2 * A_RADIUS:
        if l_sub >= 128 + 2 * A_RADIUS and l_sub % 128 == 0:
            return 128, 128 + 2 * A_RADIUS
        return l_sub, l_sub
    return 256, 256 + 2 * A_RADIUS


def _attn_a_kernel(q_ref, k_ref, v_ref, o_ref, lse_ref, *, d, n_short_blk, sp, ss):
    def rows(start, size):
        if d == 1:
            return pl.ds(start, size)
        return pl.ds(start, size, stride=d)

    def run(seq, n_seq):
        l_sub = seq // d
        tq, win = _attn_a_tiles(l_sub)
        n_tile = l_sub // tq

        def tile(idx, carry):
            base = (idx // (d * n_tile)) * seq
            r = (idx // n_tile) % d
            t0 = (idx % n_tile) * tq
            ks = jnp.clip(t0 - A_RADIUS, 0, l_sub - win)
            q = q_ref[rows(base + r + t0 * d, tq), :].astype(BF16)
            k = k_ref[rows(base + r + ks * d, win), :].astype(BF16)
            v = v_ref[rows(base + r + ks * d, win), :].astype(BF16)
            s = lax.dot_general(q, k, (((1,), (1,)), ((), ())),
                                preferred_element_type=F32) * ATTN_SCALE
            qpos = t0 + lax.broadcasted_iota(jnp.int32, (tq, win), 0)
            kpos = ks + lax.broadcasted_iota(jnp.int32, (tq, win), 1)
            s = jnp.where(jnp.abs(kpos - qpos) <= A_RADIUS, s, NEG)
            m = jnp.max(s, axis=-1, keepdims=True)
            p = jnp.exp(s - m)
            den = jnp.sum(p, axis=-1, keepdims=True)
            o = jnp.dot(p.astype(BF16), v, preferred_element_type=F32) / den
            lse = m + jnp.log(den)
            o_ref[rows(base + r + t0 * d, tq), :] = o
            lse_ref[rows(base + r + t0 * d, tq), :] = jnp.broadcast_to(lse, (tq, LANES))
            return carry

        lax.fori_loop(0, n_seq * d * n_tile, tile, 0, unroll=ATTN_A_UNROLL)

    if sp == ss:
        run(ss, 1)
    else:
        blk = pl.program_id(0)

        @pl.when(blk < n_short_blk)
        def _():
            run(sp, ss // sp)

        @pl.when(blk >= n_short_blk)
        def _():
            run(ss, 1)


def _attn_a(qk, v, g, tp, sp, ss):
    t = qk.shape[0]
    kern = functools.partial(_attn_a_kernel, d=A_DILATIONS[g], n_short_blk=tp // ss, sp=sp, ss=ss)
    return pl.pallas_call(
        kern,
        grid=(t // ss, A_HEADS),
        in_specs=[pl.BlockSpec((ss, LANES), lambda b, h: (b, g * A_HEADS + h)),
                  pl.BlockSpec((ss, LANES), lambda b, h: (b, (A_N_GROUPS + g) * A_HEADS + h)),
                  pl.BlockSpec((ss, LANES), lambda b, h: (b, g * A_HEADS + h))],
        out_specs=[pl.BlockSpec((ss, LANES), lambda b, h: (b, h))] * 2,
        out_shape=[jax.ShapeDtypeStruct((t, A_WIDTH), F32)] * 2,
        compiler_params=_cparams(2),
        name=f"attn_a{g}",
    )(qk, qk, v)


ATTN_B_TQ = 256
ATTN_B_TK = 1024


def _attn_b_kernel(qt_ref, k_ref, vt_ref, o_ref, s_ref, *, tq, tk, tp, sp, ss):
    row = pl.program_id(1) * tq
    short = row < tp
    first = jnp.where(short, ((row % ss) // sp) * sp, 0) // tk
    n_chunk = jnp.where(short, sp // tk, ss // tk)
    last = first + n_chunk - 1
    qt = jnp.concatenate([qt_ref[h * HEAD_DIM:(h + 1) * HEAD_DIM, :] for h in range(B_GROUP)],
                         axis=1)
    cols = B_GROUP * tq

    def scores(c, slot):
        k = k_ref[pl.ds(pl.multiple_of(c * tk, tk), tk), :]
        s_ref[slot] = jnp.dot(k, qt, preferred_element_type=F32)

    def chunk(c, slot, carry, prefetch=True):
        if prefetch:
            scores(c + 1, 1 - slot)
        m, l, acc = carry
        st = s_ref[slot]
        m_new = jnp.maximum(m, jnp.max(st, axis=0, keepdims=True))
        alpha = jnp.exp2(m - m_new)
        pt = jnp.exp2(st - m_new)
        l = alpha * l + jnp.sum(pt, axis=0, keepdims=True)
        acc = alpha * acc + jnp.dot(vt_ref[c], pt.astype(BF16), preferred_element_type=F32)
        return m_new, l, acc

    def body(it, carry):
        for slot in range(2):
            carry = chunk(first + 2 * it + slot, slot, carry)
        return carry

    scores(first, 0)
    init = (jnp.full((1, cols), NEG, F32), jnp.zeros((1, cols), F32),
            jnp.zeros((HEAD_DIM, cols), F32))
    carry = lax.fori_loop(0, n_chunk // 2 - 1, body, init)
    carry = chunk(last - 1, 0, carry)
    _, l, acc = chunk(last, 1, carry, prefetch=False)
    ot = acc / l
    for h in range(B_GROUP):
        o_ref[:, h * LANES:(h + 1) * LANES] = ot[:, h * tq:(h + 1) * tq].T.astype(o_ref.dtype)


def _attn_b(qt, k, vt, tp, sp, ss):
    tq, tk = ATTN_B_TQ, ATTN_B_TK
    t = k.shape[0]
    gw = B_GROUP * HEAD_DIM
    assert sp % (2 * tk) == 0 and ss % (2 * tk) == 0 and sp % tq == 0
    return pl.pallas_call(
        functools.partial(_attn_b_kernel, tq=tq, tk=tk, tp=tp, sp=sp, ss=ss),
        grid=(B_KV_HEADS, t // tq),
        scratch_shapes=[pltpu.VMEM((2, tk, gw // HEAD_DIM * tq), F32)],
        in_specs=[pl.BlockSpec((None, gw, tq), lambda kv, i: (i, kv, 0)),
                  pl.BlockSpec((ss, HEAD_DIM), lambda kv, i: (i // (ss // tq), kv)),
                  pl.BlockSpec((ss // tk, HEAD_DIM, tk), lambda kv, i: (i // (ss // tq), kv, 0))],
        out_specs=pl.BlockSpec((tq, gw), lambda kv, i: (i, kv)),
        out_shape=jax.ShapeDtypeStruct((t, B_WIDTH), BF16),
        compiler_params=_cparams(2),
        name="attn_b",
    )(qt, k, vt)


def _merge_kernel(o0, o1, o2, l0, l1, l2, ob_ref, gate_ref, wpa_ref, wpb_ref, out_ref):
    lse = [l0[...], l1[...], l2[...]]
    top = jnp.maximum(jnp.maximum(lse[0], lse[1]), lse[2])
    e = [jnp.exp(x - top) for x in lse]
    oa = (e[0] * o0[...] + e[1] * o1[...] + e[2] * o2[...]) / (e[0] + e[1] + e[2])
    pa = jnp.dot(oa.astype(BF16), wpa_ref[...], preferred_element_type=F32)
    pb = jnp.dot(ob_ref[...], wpb_ref[...], preferred_element_type=F32)
    ga = gate_ref[:, :D_MODEL].astype(F32)
    gb = gate_ref[:, D_MODEL:].astype(F32)
    out_ref[...] = (ga * pa + gb * pb).astype(out_ref.dtype)


def _merge(oa, lse, ob, gates, wpa, wpb, bm):
    t = ob.shape[0]
    row = lambda w: pl.BlockSpec((bm, w), lambda i: (i, 0))
    const = lambda s: pl.BlockSpec(s, lambda i: (0, 0))
    return pl.pallas_call(
        _merge_kernel,
        grid=(t // bm,),
        in_specs=[row(A_WIDTH)] * 6 + [row(B_WIDTH), row(C_GATE),
                                       const((A_WIDTH, D_MODEL)), const((B_WIDTH, D_MODEL))],
        out_specs=row(D_MODEL),
        out_shape=jax.ShapeDtypeStruct((t, D_MODEL), BF16),
        compiler_params=_cparams(1),
        name="merge",
    )(*oa, *lse, ob, gates, wpa, wpb)


def _ffn_kernel(xn_ref, wg_ref, wu_ref, wd_ref, x_ref, gn_ref, o_ref, on_ref, *, n_chunk):
    c = pl.program_id(1)

    @pl.when(c == 0)
    def _():
        o_ref[...] = x_ref[...]

    xn = xn_ref[...]
    g = jnp.dot(xn, wg_ref[...], preferred_element_type=F32)
    u = jnp.dot(xn, wu_ref[...], preferred_element_type=F32)
    hm = (g * jax.nn.sigmoid(g) * u).astype(BF16)
    o_ref[...] += jnp.dot(hm, wd_ref[...], preferred_element_type=F32)

    @pl.when(c == n_chunk - 1)
    def _():
        on_ref[...] = _rms(o_ref[...], gn_ref[...]).astype(on_ref.dtype)


def _ffn(xn, wg, wu, wd, x, g_next, bm, bc):
    t = xn.shape[0]
    f = wg.shape[1]
    n_chunk = f // bc
    row = lambda: pl.BlockSpec((bm, D_MODEL), lambda i, c: (i, 0))
    return pl.pallas_call(
        functools.partial(_ffn_kernel, n_chunk=n_chunk),
        grid=(t // bm, n_chunk),
        in_specs=[row(),
                  pl.BlockSpec((D_MODEL, bc), lambda i, c: (0, c)),
                  pl.BlockSpec((D_MODEL, bc), lambda i, c: (0, c)),
                  pl.BlockSpec((bc, D_MODEL), lambda i, c: (c, 0)),
                  row(),
                  pl.BlockSpec((1, D_MODEL), lambda i, c: (0, 0))],
        out_specs=[row(), row()],
        out_shape=[jax.ShapeDtypeStruct((t, D_MODEL), F32),
                   jax.ShapeDtypeStruct((t, D_MODEL), BF16)],
        compiler_params=_cparams(2),
        name="ffn_dense",
    )(xn, wg, wu, wd, x, g_next)


def _router_kernel(x_ref, g_ref, wr_ref, xn_ref, route_ref):
    xn = _rms(x_ref[...], g_ref[...])
    xn_ref[...] = xn
    logits = jnp.dot(xn, wr_ref[...], preferred_element_type=F32,
                     precision=lax.Precision.HIGHEST)
    lane = lax.broadcasted_iota(jnp.int32, logits.shape, 1)
    lg = jnp.where(lane < N_EXPERTS, logits, -jnp.inf)
    v1 = jnp.max(lg, axis=-1, keepdims=True)
    i1 = jnp.min(jnp.where(lg == v1, lane, LANES), axis=-1, keepdims=True)
    lg2 = jnp.where(lane == i1, -jnp.inf, lg)
    v2 = jnp.max(lg2, axis=-1, keepdims=True)
    i2 = jnp.min(jnp.where(lg2 == v2, lane, LANES), axis=-1, keepdims=True)
    e2 = jnp.exp(v2 - v1)
    w1 = 1.0 / (1.0 + e2)
    w2 = e2 / (1.0 + e2)
    route = jnp.where(lane == 0, i1.astype(F32),
                      jnp.where(lane == 1, i2.astype(F32),
                                jnp.where(lane == 2, w1, jnp.where(lane == 3, w2, 0.0))))
    route_ref[...] = route


def _router(x, g, wr_pad, bm):
    t = x.shape[0]
    return pl.pallas_call(
        _router_kernel,
        grid=(t // bm,),
        in_specs=[pl.BlockSpec((bm, D_MODEL), lambda i: (i, 0)),
                  pl.BlockSpec((1, D_MODEL), lambda i: (0, 0)),
                  pl.BlockSpec((D_MODEL, LANES), lambda i: (0, 0))],
        out_specs=[pl.BlockSpec((bm, D_MODEL), lambda i: (i, 0)),
                   pl.BlockSpec((bm, LANES), lambda i: (i, 0))],
        out_shape=[jax.ShapeDtypeStruct((t, D_MODEL), F32),
                   jax.ShapeDtypeStruct((t, LANES), F32)],
        compiler_params=_cparams(1),
        name="router",
    )(x, g, wr_pad)


def _moe_kernel(be_ref, nvalid_ref,
                tok_ref, dest_ref, xn_hbm, wg_ref, wu_ref, wd_ref, wb_ref, y_hbm,
                xs_f32, xs_bf, acc, sem_in, sem_out, *, bm, n_chunk):
    i = pl.program_id(0)
    c = pl.program_id(1)
    n_valid = nvalid_ref[i]
    used = n_valid > 0

    def row_in(r, t):
        return pltpu.make_async_copy(xn_hbm.at[pl.ds(t, 1), :], xs_f32.at[pl.ds(r, 1), :], sem_in)

    def row_out(r, t):
        return pltpu.make_async_copy(acc.at[pl.ds(r, 1), :], y_hbm.at[pl.ds(t, 1), :], sem_out)

    @pl.when(jnp.logical_and(used, c == 0))
    def _():
        def start(r, carry):
            row_in(r, tok_ref[0, r]).start()
            return carry

        def wait(r, carry):
            row_in(r, 0).wait()
            return carry

        lax.fori_loop(0, bm, start, 0)
        lax.fori_loop(0, bm, wait, 0)
        xs_bf[...] = xs_f32[...].astype(BF16)

    @pl.when(used)
    def _():
        xs = xs_bf[...]
        g = jnp.dot(xs, wg_ref[...], preferred_element_type=F32)
        u = jnp.dot(xs, wu_ref[...], preferred_element_type=F32)
        hm = (g * jax.nn.sigmoid(g) * u).astype(BF16)
        part = jnp.dot(hm, wd_ref[...], preferred_element_type=F32)

        @pl.when(c == 0)
        def _():
            acc[...] = part

        @pl.when(c > 0)
        def _():
            acc[...] += part

    @pl.when(jnp.logical_and(used, c == n_chunk - 1))
    def _():
        acc[...] = acc[...] * jnp.tile(wb_ref[...], (1, D_MODEL // LANES))

        def start(r, carry):
            row_out(r, dest_ref[0, r]).start()
            return carry

        def wait(r, carry):
            row_out(r, 0).wait()
            return carry

        lax.fori_loop(0, n_valid, start, 0)
        lax.fori_loop(0, n_valid, wait, 0)


def _moe(xn, wg, wu, wd, blk_expert, slot_tok, slot_dest, blk_valid, slot_wb, n_rows_out, bm, bc):
    n_slots = slot_tok.shape[0]
    n_blocks = n_slots // bm
    f = wg.shape[2]
    n_chunk = f // bc
    idx_spec = pl.BlockSpec((None, 1, bm), lambda i, c, *_: (i, 0, 0), memory_space=pltpu.SMEM)
    grid_spec = pltpu.PrefetchScalarGridSpec(
        num_scalar_prefetch=2,
        grid=(n_blocks, n_chunk),
        in_specs=[idx_spec, idx_spec,
                  pl.BlockSpec(memory_space=pl.ANY),
                  pl.BlockSpec((None, D_MODEL, bc), lambda i, c, be, *_: (be[i], 0, c)),
                  pl.BlockSpec((None, D_MODEL, bc), lambda i, c, be, *_: (be[i], 0, c)),
                  pl.BlockSpec((None, bc, D_MODEL), lambda i, c, be, *_: (be[i], c, 0)),
                  pl.BlockSpec((bm, LANES), lambda i, c, *_: (i, 0))],
        out_specs=pl.BlockSpec(memory_space=pl.ANY),
        scratch_shapes=[pltpu.VMEM((bm, D_MODEL), F32),
                        pltpu.VMEM((bm, D_MODEL), BF16),
                        pltpu.VMEM((bm, D_MODEL), F32),
                        pltpu.SemaphoreType.DMA(()),
                        pltpu.SemaphoreType.DMA(())],
    )
    return pl.pallas_call(
        functools.partial(_moe_kernel, bm=bm, n_chunk=n_chunk),
        grid_spec=grid_spec,
        out_shape=jax.ShapeDtypeStruct((n_rows_out, D_MODEL), F32),
        compiler_params=_cparams(2),
        name="moe_experts",
    )(blk_expert, blk_valid, slot_tok.reshape(n_blocks, 1, bm), slot_dest.reshape(n_blocks, 1, bm),
      xn, wg, wu, wd, slot_wb)


def _final_kernel(x_ref, y0_ref, y1_ref, g_ref, o_ref):
    x = x_ref[...] + (y0_ref[...] + y1_ref[...])
    o_ref[...] = _rms(x, g_ref[...])


def _final(x, y, g, row0, rows, t, bm):
    rb, tb = row0 // bm, t // bm
    return pl.pallas_call(
        _final_kernel,
        grid=(rows // bm,),
        in_specs=[pl.BlockSpec((bm, D_MODEL), lambda i: (rb + i, 0)),
                  pl.BlockSpec((bm, D_MODEL), lambda i: (rb + i, 0)),
                  pl.BlockSpec((bm, D_MODEL), lambda i: (tb + rb + i, 0)),
                  pl.BlockSpec((1, D_MODEL), lambda i: (0, 0))],
        out_specs=pl.BlockSpec((bm, D_MODEL), lambda i: (i, 0)),
        out_shape=jax.ShapeDtypeStruct((rows, D_MODEL), F32),
        compiler_params=_cparams(1),
        name="final_norm",
    )(x, y, y, g)


def _rope_tables(n_pos):
    pos = jnp.arange(n_pos, dtype=F32)
    ones = jnp.ones((n_pos, 1), F32)
    zeros = jnp.zeros((n_pos, 1), F32)

    def angles(p, dims, theta):
        inv = jnp.power(jnp.float32(theta), -jnp.arange(0, dims, 2, dtype=F32) / dims)
        return p[:, None] * inv[None, :]

    ang = angles(pos, ROPE_DIMS, ROPE_THETA)
    rest = HEAD_DIM - ROPE_DIMS
    cos_a = jnp.concatenate([jnp.cos(ang), jnp.cos(ang), jnp.tile(ones, (1, rest))], axis=1)
    half = jnp.tile(zeros, (1, ROPE_DIMS // 2))
    tail = jnp.tile(zeros, (1, rest))
    lo_a = jnp.concatenate([-jnp.sin(ang), half, tail], axis=1)
    hi_a = jnp.concatenate([half, jnp.sin(ang), tail], axis=1)
    row = jnp.floor(pos / GRID_W)
    col = pos - row * GRID_W
    ar = angles(row, HEAD_DIM // 2, AXIAL_THETA)
    ac = angles(col, HEAD_DIM // 2, AXIAL_THETA)
    quarter = jnp.tile(zeros, (1, HEAD_DIM // 4))
    cos_b = jnp.concatenate([jnp.cos(ar), jnp.cos(ar), jnp.cos(ac), jnp.cos(ac)], axis=1)
    lo_b = jnp.concatenate([-jnp.sin(ar), quarter, -jnp.sin(ac), quarter], axis=1)
    hi_b = jnp.concatenate([quarter, jnp.sin(ar), quarter, jnp.sin(ac)], axis=1)
    return (cos_a, lo_a, hi_a), (cos_b, lo_b, hi_b)


def _routing_slots(route, t, bm):
    n_asg = t * TOP_K
    flat_e = route[:, :TOP_K].astype(jnp.int32).reshape(-1)
    flat_w = route[:, TOP_K:2 * TOP_K].reshape(-1)
    onehot = (flat_e[:, None] == jnp.arange(N_EXPERTS, dtype=jnp.int32)[None, :]).astype(jnp.int32)
    csum = jnp.cumsum(onehot, axis=0)
    counts = csum[-1]
    rank = jnp.take_along_axis(csum, flat_e[:, None], axis=1)[:, 0] - 1
    padded = (counts + bm - 1) // bm * bm
    padded_end = jnp.cumsum(padded)
    padded_start = padded_end - padded
    slot = padded_start[flat_e] + rank
    n_slots = n_asg + N_EXPERTS * bm
    n_blocks = n_slots // bm
    asg = jnp.arange(n_asg, dtype=jnp.int32)
    tok = asg // TOP_K
    packed = jnp.stack([tok, (asg % TOP_K) * t + tok, lax.bitcast_convert_type(flat_w, jnp.int32)],
                       axis=1)
    slots = jnp.zeros((n_slots, 3), jnp.int32).at[slot].set(packed)
    slot_tok, slot_dest = slots[:, 0], slots[:, 1]
    slot_w = lax.bitcast_convert_type(slots[:, 2], F32)
    blk_start = jnp.arange(n_blocks, dtype=jnp.int32) * bm
    blk_expert = jnp.minimum(jnp.searchsorted(padded_end, blk_start, side='right'),
                             N_EXPERTS - 1).astype(jnp.int32)
    blk_valid = jnp.clip(padded_start[blk_expert] + counts[blk_expert] - blk_start, 0, bm)
    blk_valid = jnp.where(blk_start < padded_end[-1], blk_valid, 0).astype(jnp.int32)
    slot_wb = jnp.broadcast_to(slot_w[:, None], (n_slots, LANES))
    return blk_expert, slot_tok, slot_dest, blk_valid, slot_wb


def _tile(n, pref):
    while n % pref:
        pref //= 2
    return pref


def _trunk(x_prompt, x_sample, attn_norm, w_in, b_gate, q_norm, k_norm, w_proj_a, w_proj_b,
           w_out, ffn_norm, ffn_w_gate, ffn_w_up, ffn_w_down, moe_router, moe_w_gate,
           moe_w_up, moe_w_down, final_norm):
    bp, sp, _ = x_prompt.shape
    bs, ss, _ = x_sample.shape
    tp, ts = bp * sp, bs * ss
    t = tp + ts
    depth = w_in.shape[0]
    assert depth == 2, "dense-FFN layer followed by a final MoE layer"
    assert ss % sp == 0 and tp % ss == 0, "blocks of ss tokens must hold whole sequences"
    sets = ((0, tp), (tp, ts))

    seq_unit = math.gcd(sp, ss)
    bm = _tile(seq_unit, 1024)
    bm_small = _tile(seq_unit, 512)
    n_p_blk, n_pb, n_sb = tp // bm, sp // bm, ss // bm

    def pos_blk(i, j):
        return (jnp.where(i < n_p_blk, i % n_pb, (i - n_p_blk) % n_sb), 0)

    tab_a, tab_b = _rope_tables(max(sp, ss))

    row2 = lambda v: v.reshape(1, -1).astype(F32)
    x, xn = _norm_in(x_prompt.reshape(tp, D_MODEL), x_sample.reshape(ts, D_MODEL),
                     row2(attn_norm[0]), bm_small)

    out = None
    for layer in range(depth):
        qk_a, v_a, q_b, k_b, v_b, gates = _proj_in(
            xn, w_in[layer].astype(BF16), tab_a, tab_b, row2(q_norm[layer]), row2(k_norm[layer]),
            row2(b_gate[layer]), pos_blk, bm)

        oa, lse = [], []
        for g in range(A_N_GROUPS):
            o_g, lse_g = _attn_a(qk_a, v_a, g, tp, sp, ss)
            oa.append(o_g)
            lse.append(lse_g)
        ob = _attn_b(q_b, k_b, v_b, tp, sp, ss)

        merged = _merge(oa, lse, ob, gates, w_proj_a[layer].astype(BF16),
                        w_proj_b[layer].astype(BF16), bm_small // 2)
        x, xn_ffn = _matmul(
            merged, w_out[layer].astype(BF16), bm=bm_small, bn=D_MODEL,
            epilogue=_epi_residual_norm,
            extras=[x, row2(ffn_norm[layer])],
            extra_specs=[pl.BlockSpec((bm_small, D_MODEL), lambda i, j: (i, 0)),
                         pl.BlockSpec((1, D_MODEL), lambda i, j: (0, 0))],
            out_shape=[jax.ShapeDtypeStruct((t, D_MODEL), F32),
                       jax.ShapeDtypeStruct((t, D_MODEL), BF16)],
            out_specs=[pl.BlockSpec((bm_small, D_MODEL), lambda i, j: (i, 0))] * 2,
            name="proj_out")

        j = layer // 2
        if layer % 2 == 0:
            x, xn = _ffn(xn_ffn, ffn_w_gate[j].astype(BF16), ffn_w_up[j].astype(BF16),
                         ffn_w_down[j].astype(BF16), x, row2(attn_norm[layer + 1]), bm_small, 512)
        else:
            wr = jnp.pad(moe_router[j].astype(F32), ((0, 0), (0, LANES - N_EXPERTS)))
            xn_moe, route = _router(x, row2(ffn_norm[layer]), wr, bm_small)
            meta = _routing_slots(route, t, bm_small)
            n_rows_out = TOP_K * t
            y = _moe(xn_moe, moe_w_gate[j].astype(BF16), moe_w_up[j].astype(BF16),
                     moe_w_down[j].astype(BF16), *meta, n_rows_out, bm_small, 512)
            out = tuple(_final(x, y, row2(final_norm), row0, rows, t, bm_small)
                        for row0, rows in sets)
    y_prompt, y_sample = out
    return y_prompt.reshape(bp, sp, D_MODEL), y_sample.reshape(bs, ss, D_MODEL)


def kernel(x_prompt, x_sample, attn_norm, w_in, b_gate, q_norm, k_norm, w_proj_a, w_proj_b,
           w_out, ffn_norm, ffn_w_gate, ffn_w_up, ffn_w_down, moe_router, moe_w_gate,
           moe_w_up, moe_w_down, final_norm):
    return _trunk(x_prompt, x_sample, attn_norm, w_in, b_gate, q_norm, k_norm, w_proj_a,
                  w_proj_b, w_out, ffn_norm, ffn_w_gate, ffn_w_up, ffn_w_down, moe_router,
                  moe_w_gate, moe_w_up, moe_w_down, final_norm)
```

```python
import functools
import math

import jax
import jax.numpy as jnp
from jax import lax
from jax.experimental import pallas as pl
from jax.experimental.pallas import tpu as pltpu

F32 = jnp.float32
BF16 = jnp.bfloat16

D_MODEL = 2048
HEAD_DIM = 128
ATTN_SCALE = HEAD_DIM ** -0.5
EPS = 1e-6
NEG = -1e30

A_WINDOWS = (128, 512, 2048)
A_DILATIONS = (1, 4, 16)
A_N_GROUPS = 3
A_HEADS = 4
A_WIDTH = A_HEADS * HEAD_DIM
A_QK = 2 * A_N_GROUPS * A_WIDTH
A_QKV = 3 * A_N_GROUPS * A_WIDTH
A_RADIUS = 64
ROPE_THETA = 500000.0
ROPE_DIMS = HEAD_DIM // 4

B_Q_HEADS = 8
B_KV_HEADS = 2
B_GROUP = B_Q_HEADS // B_KV_HEADS
B_WIDTH = B_Q_HEADS * HEAD_DIM
B_KVW = B_KV_HEADS * HEAD_DIM
B_QKV = B_WIDTH + 2 * B_KVW
AXIAL_THETA = 10000.0
GRID_W = 64

C_GATE = 2 * D_MODEL
N_EXPERTS = 8
TOP_K = 2

V7X_VMEM_BYTES = 64 * 1024 * 1024
VMEM_LIMIT = V7X_VMEM_BYTES - 8 * 1024 * 1024
LANES = 128

LOG2E = math.log2(math.e)


def _cparams(n_axes, vmem=VMEM_LIMIT):
    return pltpu.CompilerParams(dimension_semantics=("arbitrary",) * n_axes,
                                vmem_limit_bytes=vmem)


def _rms(x, g):
    return x * lax.rsqrt(jnp.mean(x * x, axis=-1, keepdims=True) + EPS) * g


def _rope128(x, c, s_lo, s_hi, w):
    return x * c + pltpu.roll(x, LANES - w, 1) * s_lo + pltpu.roll(x, w, 1) * s_hi


def _norm_in_kernel(xp_ref, xs_ref, g_ref, x_ref, xn_ref, *, n_p):
    i = pl.program_id(0)

    def emit(src):
        x = src[...]
        x_ref[...] = x
        xn_ref[...] = _rms(x, g_ref[...]).astype(BF16)

    @pl.when(i < n_p)
    def _():
        emit(xp_ref)

    @pl.when(i >= n_p)
    def _():
        emit(xs_ref)


def _norm_in(xp, xs, g, bm):
    tp, ts = xp.shape[0], xs.shape[0]
    n_p, n_s = tp // bm, ts // bm
    t = tp + ts
    return pl.pallas_call(
        functools.partial(_norm_in_kernel, n_p=n_p),
        grid=(n_p + n_s,),
        in_specs=[pl.BlockSpec((bm, D_MODEL), lambda i: (jnp.minimum(i, n_p - 1), 0)),
                  pl.BlockSpec((bm, D_MODEL), lambda i: (jnp.maximum(i - n_p, 0), 0)),
                  pl.BlockSpec((1, D_MODEL), lambda i: (0, 0))],
        out_specs=[pl.BlockSpec((bm, D_MODEL), lambda i: (i, 0)),
                   pl.BlockSpec((bm, D_MODEL), lambda i: (i, 0))],
        out_shape=[jax.ShapeDtypeStruct((t, D_MODEL), F32),
                   jax.ShapeDtypeStruct((t, D_MODEL), BF16)],
        compiler_params=_cparams(1),
        name="norm_in",
    )(xp, xs, g)


def _mm_kernel(a_ref, w_ref, *refs, epilogue, n_extra):
    acc = jnp.dot(a_ref[...], w_ref[...], preferred_element_type=F32)
    epilogue(acc, refs[:n_extra], refs[n_extra:])


def _matmul(a, w, *, bm, bn, epilogue, extras=(), extra_specs=(), out_shape, out_specs, name):
    m, k = a.shape
    n = w.shape[1]
    assert m % bm == 0 and n % bn == 0, (m, n, bm, bn)
    return pl.pallas_call(
        functools.partial(_mm_kernel, epilogue=epilogue, n_extra=len(extras)),
        grid=(m // bm, n // bn),
        in_specs=[pl.BlockSpec((bm, k), lambda i, j: (i, 0)),
                  pl.BlockSpec((k, bn), lambda i, j: (0, j)),
                  *extra_specs],
        out_specs=out_specs,
        out_shape=out_shape,
        compiler_params=_cparams(2),
        name=name,
    )(a, w, *extras)


def _epi_residual_norm(acc, extras, outs):
    x = extras[0][...] + acc
    outs[0][...] = x
    outs[1][...] = _rms(x, extras[1][...]).astype(outs[1].dtype)


PROJ_BN = 2 * B_KVW
_J_VA = A_QK // PROJ_BN
_J_QB = A_QKV // PROJ_BN
_J_KV = (A_QKV + B_WIDTH) // PROJ_BN
_J_GATE = (A_QKV + B_QKV) // PROJ_BN
_J_END = (A_QKV + B_QKV + C_GATE) // PROJ_BN


def _proj_in_kernel(xn_ref, w_ref, ca_ref, la_ref, ha_ref, cb_ref, lb_ref, hb_ref,
                    gq_ref, gk_ref, bias_ref,
                    qka_ref, va_ref, qb_ref, kb_ref, vb_ref, gate_ref, acc_ref):
    j = pl.program_id(1)
    p = j - 1

    def matmul():
        acc_ref[...] = jnp.dot(xn_ref[...], w_ref[...], preferred_element_type=F32)

    def heads(n):
        return [slice(h * LANES, (h + 1) * LANES) for h in range(n // LANES)]

    def norm_rope_b(x, g_ref):
        return _rope128(_rms(x, g_ref[...]), cb_ref[...], lb_ref[...], hb_ref[...], HEAD_DIM // 4)

    def epi_qk_a():
        for sl in heads(PROJ_BN):
            qka_ref[:, sl] = _rope128(acc_ref[:, sl], ca_ref[...], la_ref[...], ha_ref[...],
                                      ROPE_DIMS // 2)

    def epi_v_a():
        va_ref[...] = acc_ref[...]

    def epi_q_b():
        tq = qb_ref.shape[2]
        for h, sl in enumerate(heads(PROJ_BN)):
            yt = (norm_rope_b(acc_ref[:, sl], gq_ref) * (ATTN_SCALE * LOG2E)).T
            for n in range(qb_ref.shape[0]):
                qb_ref[n, h * HEAD_DIM:(h + 1) * HEAD_DIM, :] = (
                    yt[:, n * tq:(n + 1) * tq].astype(BF16))

    def epi_kv_b():
        for sl in heads(B_KVW):
            kb_ref[:, sl] = norm_rope_b(acc_ref[:, sl], gk_ref).astype(BF16)
        tk = vb_ref.shape[2]
        vt = acc_ref[:, B_KVW:].T
        for n in range(vb_ref.shape[0]):
            vb_ref[n] = vt[:, n * tk:(n + 1) * tk].astype(BF16)

    def epi_gate():
        gate_ref[...] = jax.nn.sigmoid(acc_ref[...] + bias_ref[...]).astype(BF16)

    def between(lo, hi):
        return jnp.logical_and(p >= lo, p < hi)

    @pl.when(j == 0)
    def _():
        matmul()

    for lo, hi, epilogue in ((0, _J_VA, epi_qk_a), (_J_VA, _J_QB, epi_v_a), (_J_QB, _J_KV, epi_q_b),
                             (_J_KV, _J_GATE, epi_kv_b), (_J_GATE, _J_END - 1, epi_gate)):
        @pl.when(between(lo, hi))
        def _(epilogue=epilogue):
            epilogue()
            matmul()

    @pl.when(p == _J_END - 1)
    def _():
        epi_gate()


def _proj_in(xn, w, tab_a, tab_b, gq, gk, bias, pos_blk, bm):
    t = xn.shape[0]
    bn = PROJ_BN
    tq, tk = ATTN_B_TQ, ATTN_B_TK
    assert bm % tq == 0 and bm % tk == 0

    def cols(lo, hi):
        return lambda i, j: (i, jnp.clip(j - 1 - lo, 0, hi - lo - 1))

    tab_spec = pl.BlockSpec((bm, LANES), pos_blk)
    vec_spec = pl.BlockSpec((1, LANES), lambda i, j: (0, 0))
    return pl.pallas_call(
        _proj_in_kernel,
        grid=(t // bm, _J_END + 1),
        in_specs=[pl.BlockSpec((bm, D_MODEL), lambda i, j: (i, 0)),
                  pl.BlockSpec((D_MODEL, bn), lambda i, j: (0, jnp.minimum(j, _J_END - 1))),
                  *([tab_spec] * 6), vec_spec, vec_spec,
                  pl.BlockSpec((1, bn), lambda i, j: (0, jnp.clip(j - 1 - _J_GATE, 0, _J_END - _J_GATE - 1)))],
        out_specs=[pl.BlockSpec((bm, bn), cols(0, _J_VA)),
                   pl.BlockSpec((bm, bn), cols(_J_VA, _J_QB)),
                   pl.BlockSpec((bm // tq, bn, tq),
                                lambda i, j: (i, jnp.clip(j - 1 - _J_QB, 0, _J_KV - _J_QB - 1), 0)),
                   pl.BlockSpec((bm, B_KVW), lambda i, j: (i, 0)),
                   pl.BlockSpec((bm // tk, B_KVW, tk), lambda i, j: (i, 0, 0)),
                   pl.BlockSpec((bm, bn), cols(_J_GATE, _J_END))],
        out_shape=[jax.ShapeDtypeStruct((t, A_QK), F32),
                   jax.ShapeDtypeStruct((t, A_QKV - A_QK), F32),
                   jax.ShapeDtypeStruct((t // tq, B_WIDTH, tq), BF16),
                   jax.ShapeDtypeStruct((t, B_KVW), BF16),
                   jax.ShapeDtypeStruct((t // tk, B_KVW, tk), BF16),
                   jax.ShapeDtypeStruct((t, C_GATE), BF16)],
        scratch_shapes=[pltpu.VMEM((bm, bn), F32)],
        compiler_params=_cparams(2),
        name="proj_in",
    )(xn, w, *tab_a, *tab_b, gq, gk, bias)


ATTN_A_UNROLL = 4


def _attn_a_tiles(l_sub):
    if l_sub <= 256 + 2 * A_RADIUS:
        if l_sub >= 128 + 2 * A_RADIUS and l_sub % 128 == 0:
            return 128, 128 + 2 * A_RADIUS
        return l_sub, l_sub
    return 256, 256 + 2 * A_RADIUS


def _attn_a_kernel(q_ref, k_ref, v_ref, o_ref, lse_ref, *, d, n_short_blk, sp, ss):
    def rows(start, size):
        if d == 1:
            return pl.ds(start, size)
        return pl.ds(start, size, stride=d)

    def run(seq, n_seq):
        l_sub = seq // d
        tq, win = _attn_a_tiles(l_sub)
        n_tile = l_sub // tq

        def tile(idx, carry):
            base = (idx // (d * n_tile)) * seq
            r = (idx // n_tile) % d
            t0 = (idx % n_tile) * tq
            ks = jnp.clip(t0 - A_RADIUS, 0, l_sub - win)
            q = q_ref[rows(base + r + t0 * d, tq), :].astype(BF16)
            k = k_ref[rows(base + r + ks * d, win), :].astype(BF16)
            v = v_ref[rows(base + r + ks * d, win), :].astype(BF16)
            s = lax.dot_general(q, k, (((1,), (1,)), ((), ())),
                                preferred_element_type=F32) * ATTN_SCALE
            qpos = t0 + lax.broadcasted_iota(jnp.int32, (tq, win), 0)
            kpos = ks + lax.broadcasted_iota(jnp.int32, (tq, win), 1)
            s = jnp.where(jnp.abs(kpos - qpos) <= A_RADIUS, s, NEG)
            m = jnp.max(s, axis=-1, keepdims=True)
            p = jnp.exp(s - m)
            den = jnp.sum(p, axis=-1, keepdims=True)
            o = jnp.dot(p.astype(BF16), v, preferred_element_type=F32) / den
            lse = m + jnp.log(den)
            o_ref[rows(base + r + t0 * d, tq), :] = o
            lse_ref[rows(base + r + t0 * d, tq), :] = jnp.broadcast_to(lse, (tq, LANES))
            return carry

        lax.fori_loop(0, n_seq * d * n_tile, tile, 0, unroll=ATTN_A_UNROLL)

    if sp == ss:
        run(ss, 1)
    else:
        blk = pl.program_id(0)

        @pl.when(blk < n_short_blk)
        def _():
            run(sp, ss // sp)

        @pl.when(blk >= n_short_blk)
        def _():
            run(ss, 1)


def _attn_a(qk, v, g, tp, sp, ss):
    t = qk.shape[0]
    kern = functools.partial(_attn_a_kernel, d=A_DILATIONS[g], n_short_blk=tp // ss, sp=sp, ss=ss)
    return pl.pallas_call(
        kern,
        grid=(t // ss, A_HEADS),
        in_specs=[pl.BlockSpec((ss, LANES), lambda b, h: (b, g * A_HEADS + h)),
                  pl.BlockSpec((ss, LANES), lambda b, h: (b, (A_N_GROUPS + g) * A_HEADS + h)),
                  pl.BlockSpec((ss, LANES), lambda b, h: (b, g * A_HEADS + h))],
        out_specs=[pl.BlockSpec((ss, LANES), lambda b, h: (b, h))] * 2,
        out_shape=[jax.ShapeDtypeStruct((t, A_WIDTH), F32)] * 2,
        compiler_params=_cparams(2),
        name=f"attn_a{g}",
    )(qk, qk, v)


ATTN_B_TQ = 256
ATTN_B_TK = 1024


def _attn_b_kernel(qt_ref, k_ref, vt_ref, o_ref, s_ref, *, tq, tk, tp, sp, ss):
    row = pl.program_id(1) * tq
    short = row < tp
    first = jnp.where(short, ((row % ss) // sp) * sp, 0) // tk
    n_chunk = jnp.where(short, sp // tk, ss // tk)
    last = first + n_chunk - 1
    qt = jnp.concatenate([qt_ref[h * HEAD_DIM:(h + 1) * HEAD_DIM, :] for h in range(B_GROUP)],
                         axis=1)
    cols = B_GROUP * tq

    def scores(c, slot):
        k = k_ref[pl.ds(pl.multiple_of(c * tk, tk), tk), :]
        s_ref[slot] = jnp.dot(k, qt, preferred_element_type=F32)

    def chunk(c, slot, carry, prefetch=True):
        if prefetch:
            scores(c + 1, 1 - slot)
        m, l, acc = carry
        st = s_ref[slot]
        m_new = jnp.maximum(m, jnp.max(st, axis=0, keepdims=True))
        alpha = jnp.exp2(m - m_new)
        pt = jnp.exp2(st - m_new)
        l = alpha * l + jnp.sum(pt, axis=0, keepdims=True)
        acc = alpha * acc + jnp.dot(vt_ref[c], pt.astype(BF16), preferred_element_type=F32)
        return m_new, l, acc

    def body(it, carry):
        for slot in range(2):
            carry = chunk(first + 2 * it + slot, slot, carry)
        return carry

    scores(first, 0)
    init = (jnp.full((1, cols), NEG, F32), jnp.zeros((1, cols), F32),
            jnp.zeros((HEAD_DIM, cols), F32))
    carry = lax.fori_loop(0, n_chunk // 2 - 1, body, init)
    carry = chunk(last - 1, 0, carry)
    _, l, acc = chunk(last, 1, carry, prefetch=False)
    ot = acc / l
    for h in range(B_GROUP):
        o_ref[:, h * LANES:(h + 1) * LANES] = ot[:, h * tq:(h + 1) * tq].T.astype(o_ref.dtype)


def _attn_b(qt, k, vt, tp, sp, ss):
    tq, tk = ATTN_B_TQ, ATTN_B_TK
    t = k.shape[0]
    gw = B_GROUP * HEAD_DIM
    assert sp % (2 * tk) == 0 and ss % (2 * tk) == 0 and sp % tq == 0
    return pl.pallas_call(
        functools.partial(_attn_b_kernel, tq=tq, tk=tk, tp=tp, sp=sp, ss=ss),
        grid=(B_KV_HEADS, t // tq),
        scratch_shapes=[pltpu.VMEM((2, tk, gw // HEAD_DIM * tq), F32)],
        in_specs=[pl.BlockSpec((None, gw, tq), lambda kv, i: (i, kv, 0)),
                  pl.BlockSpec((ss, HEAD_DIM), lambda kv, i: (i // (ss // tq), kv)),
                  pl.BlockSpec((ss // tk, HEAD_DIM, tk), lambda kv, i: (i // (ss // tq), kv, 0))],
        out_specs=pl.BlockSpec((tq, gw), lambda kv, i: (i, kv)),
        out_shape=jax.ShapeDtypeStruct((t, B_WIDTH), BF16),
        compiler_params=_cparams(2),
        name="attn_b",
    )(qt, k, vt)


def _merge_kernel(o0, o1, o2, l0, l1, l2, ob_ref, gate_ref, wpa_ref, wpb_ref, out_ref):
    lse = [l0[...], l1[...], l2[...]]
    top = jnp.maximum(jnp.maximum(lse[0], lse[1]), lse[2])
    e = [jnp.exp(x - top) for x in lse]
    oa = (e[0] * o0[...] + e[1] * o1[...] + e[2] * o2[...]) / (e[0] + e[1] + e[2])
    pa = jnp.dot(oa.astype(BF16), wpa_ref[...], preferred_element_type=F32)
    pb = jnp.dot(ob_ref[...], wpb_ref[...], preferred_element_type=F32)
    ga = gate_ref[:, :D_MODEL].astype(F32)
    gb = gate_ref[:, D_MODEL:].astype(F32)
    out_ref[...] = (ga * pa + gb * pb).astype(out_ref.dtype)


def _merge(oa, lse, ob, gates, wpa, wpb, bm):
    t = ob.shape[0]
    row = lambda w: pl.BlockSpec((bm, w), lambda i: (i, 0))
    const = lambda s: pl.BlockSpec(s, lambda i: (0, 0))
    return pl.pallas_call(
        _merge_kernel,
        grid=(t // bm,),
        in_specs=[row(A_WIDTH)] * 6 + [row(B_WIDTH), row(C_GATE),
                                       const((A_WIDTH, D_MODEL)), const((B_WIDTH, D_MODEL))],
        out_specs=row(D_MODEL),
        out_shape=jax.ShapeDtypeStruct((t, D_MODEL), BF16),
        compiler_params=_cparams(1),
        name="merge",
    )(*oa, *lse, ob, gates, wpa, wpb)


def _ffn_kernel(xn_ref, wg_ref, wu_ref, wd_ref, x_ref, gn_ref, o_ref, on_ref, *, n_chunk):
    c = pl.program_id(1)

    @pl.when(c == 0)
    def _():
        o_ref[...] = x_ref[...]

    xn = xn_ref[...]
    g = jnp.dot(xn, wg_ref[...], preferred_element_type=F32)
    u = jnp.dot(xn, wu_ref[...], preferred_element_type=F32)
    hm = (g * jax.nn.sigmoid(g) * u).astype(BF16)
    o_ref[...] += jnp.dot(hm, wd_ref[...], preferred_element_type=F32)

    @pl.when(c == n_chunk - 1)
    def _():
        on_ref[...] = _rms(o_ref[...], gn_ref[...]).astype(on_ref.dtype)


def _ffn(xn, wg, wu, wd, x, g_next, bm, bc):
    t = xn.shape[0]
    f = wg.shape[1]
    n_chunk = f // bc
    row = lambda: pl.BlockSpec((bm, D_MODEL), lambda i, c: (i, 0))
    return pl.pallas_call(
        functools.partial(_ffn_kernel, n_chunk=n_chunk),
        grid=(t // bm, n_chunk),
        in_specs=[row(),
                  pl.BlockSpec((D_MODEL, bc), lambda i, c: (0, c)),
                  pl.BlockSpec((D_MODEL, bc), lambda i, c: (0, c)),
                  pl.BlockSpec((bc, D_MODEL), lambda i, c: (c, 0)),
                  row(),
                  pl.BlockSpec((1, D_MODEL), lambda i, c: (0, 0))],
        out_specs=[row(), row()],
        out_shape=[jax.ShapeDtypeStruct((t, D_MODEL), F32),
                   jax.ShapeDtypeStruct((t, D_MODEL), BF16)],
        compiler_params=_cparams(2),
        name="ffn_dense",
    )(xn, wg, wu, wd, x, g_next)


def _router_kernel(x_ref, g_ref, wr_ref, xn_ref, route_ref):
    xn = _rms(x_ref[...], g_ref[...])
    xn_ref[...] = xn
    logits = jnp.dot(xn, wr_ref[...], preferred_element_type=F32,
                     precision=lax.Precision.HIGHEST)
    lane = lax.broadcasted_iota(jnp.int32, logits.shape, 1)
    lg = jnp.where(lane < N_EXPERTS, logits, -jnp.inf)
    v1 = jnp.max(lg, axis=-1, keepdims=True)
    i1 = jnp.min(jnp.where(lg == v1, lane, LANES), axis=-1, keepdims=True)
    lg2 = jnp.where(lane == i1, -jnp.inf, lg)
    v2 = jnp.max(lg2, axis=-1, keepdims=True)
    i2 = jnp.min(jnp.where(lg2 == v2, lane, LANES), axis=-1, keepdims=True)
    e2 = jnp.exp(v2 - v1)
    w1 = 1.0 / (1.0 + e2)
    w2 = e2 / (1.0 + e2)
    route = jnp.where(lane == 0, i1.astype(F32),
                      jnp.where(lane == 1, i2.astype(F32),
                                jnp.where(lane == 2, w1, jnp.where(lane == 3, w2, 0.0))))
    route_ref[...] = route


def _router(x, g, wr_pad, bm):
    t = x.shape[0]
    return pl.pallas_call(
        _router_kernel,
        grid=(t // bm,),
        in_specs=[pl.BlockSpec((bm, D_MODEL), lambda i: (i, 0)),
                  pl.BlockSpec((1, D_MODEL), lambda i: (0, 0)),
                  pl.BlockSpec((D_MODEL, LANES), lambda i: (0, 0))],
        out_specs=[pl.BlockSpec((bm, D_MODEL), lambda i: (i, 0)),
                   pl.BlockSpec((bm, LANES), lambda i: (i, 0))],
        out_shape=[jax.ShapeDtypeStruct((t, D_MODEL), F32),
                   jax.ShapeDtypeStruct((t, LANES), F32)],
        compiler_params=_cparams(1),
        name="router",
    )(x, g, wr_pad)


MOE_BM = 672
MOE_BC = 512


def _moe_kernel(be_ref, nvalid_ref,
                tok0_ref, tok_ref, dest_ref, xn_hbm, wg_ref, wu_ref, wd_ref, wb_ref, y_hbm,
                xs_f32, xs_bf, acc, sem_in, sem_out, *, bm, n_chunk, n_blocks):
    i = pl.program_id(0)
    c = pl.program_id(1)
    rows_per_step = bm // n_chunk
    is_block = i < n_blocks
    used = jnp.logical_and(is_block, nvalid_ref[jnp.minimum(i, n_blocks - 1)] > 0)
    gathers_next = i + 1 < n_blocks
    scatters_prev = i >= 1
    cur = i % 2

    def row_in(r, t):
        return pltpu.make_async_copy(xn_hbm.at[pl.ds(t, 1), :], xs_f32.at[pl.ds(r, 1), :], sem_in)

    def row_out(slot, r, t):
        return pltpu.make_async_copy(acc.at[slot, pl.ds(r, 1), :], y_hbm.at[pl.ds(t, 1), :], sem_out)

    def wait_all_in():
        pltpu.make_async_copy(xn_hbm.at[pl.ds(0, bm), :], xs_f32, sem_in).wait()

    def wait_all_out():
        pltpu.make_async_copy(acc.at[0], y_hbm.at[pl.ds(0, bm), :], sem_out).wait()

    def issue_gather():
        for k in range(rows_per_step):
            r = c * rows_per_step + k
            row_in(r, tok_ref[0, r]).start()

    def issue_scatter():
        for k in range(rows_per_step):
            r = c * rows_per_step + k
            row_out(1 - cur, r, dest_ref[0, r]).start()

    def compute():
        xs = xs_bf[...]
        g = jnp.dot(xs, wg_ref[...], preferred_element_type=F32)
        u = jnp.dot(xs, wu_ref[...], preferred_element_type=F32)
        hm = (g * jax.nn.sigmoid(g) * u).astype(BF16)
        part = jnp.dot(hm, wd_ref[...], preferred_element_type=F32)
        acc[cur] += part

    @pl.when(c == 0)
    def _():
        @pl.when(i == 0)
        def _():
            def start(r, carry):
                row_in(r, tok0_ref[0, r]).start()
                return carry
            lax.fori_loop(0, bm, start, 0)

        @pl.when(is_block)
        def _():
            wait_all_in()
            xs_bf[...] = xs_f32[...].astype(BF16)

        @pl.when(i >= 2)
        def _():
            wait_all_out()

        @pl.when(used)
        def _():
            acc[cur] = jnp.zeros((bm, D_MODEL), F32)

    fast = jnp.logical_and(jnp.logical_and(used, gathers_next), scatters_prev)

    @pl.when(fast)
    def _():
        issue_gather()
        issue_scatter()
        compute()

    @pl.when(jnp.logical_not(fast))
    def _():
        @pl.when(gathers_next)
        def _():
            issue_gather()

        @pl.when(scatters_prev)
        def _():
            issue_scatter()

        @pl.when(used)
        def _():
            compute()

    @pl.when(jnp.logical_and(used, c == n_chunk - 1))
    def _():
        acc[cur] = acc[cur] * jnp.tile(wb_ref[...], (1, D_MODEL // LANES))

    @pl.when(jnp.logical_and(i == n_blocks, c == n_chunk - 1))
    def _():
        wait_all_out()


def _moe(xn, wg, wu, wd, blk_expert, slot_tok, slot_dest, blk_valid, slot_wb, n_rows_out, bm, bc):
    n_slots = slot_tok.shape[0]
    n_blocks = n_slots // bm
    f = wg.shape[2]
    n_chunk = f // bc
    assert bm % n_chunk == 0 and n_blocks >= 2
    last = n_blocks - 1
    tok0_spec = pl.BlockSpec((None, 1, bm), lambda i, c, *_: (0, 0, 0), memory_space=pltpu.SMEM)
    tok_spec = pl.BlockSpec((None, 1, bm), lambda i, c, *_: (jnp.minimum(i + 1, last), 0, 0),
                            memory_space=pltpu.SMEM)
    dest_spec = pl.BlockSpec((None, 1, bm), lambda i, c, *_: (jnp.maximum(i - 1, 0), 0, 0),
                             memory_space=pltpu.SMEM)
    blk = lambda i: jnp.minimum(i, last)
    grid_spec = pltpu.PrefetchScalarGridSpec(
        num_scalar_prefetch=2,
        grid=(n_blocks + 1, n_chunk),
        in_specs=[tok0_spec, tok_spec, dest_spec,
                  pl.BlockSpec(memory_space=pl.ANY),
                  pl.BlockSpec((None, D_MODEL, bc), lambda i, c, be, *_: (be[blk(i)], 0, c)),
                  pl.BlockSpec((None, D_MODEL, bc), lambda i, c, be, *_: (be[blk(i)], 0, c)),
                  pl.BlockSpec((None, bc, D_MODEL), lambda i, c, be, *_: (be[blk(i)], c, 0)),
                  pl.BlockSpec((bm, LANES), lambda i, c, *_: (blk(i), 0))],
        out_specs=pl.BlockSpec(memory_space=pl.ANY),
        scratch_shapes=[pltpu.VMEM((bm, D_MODEL), F32),
                        pltpu.VMEM((bm, D_MODEL), BF16),
                        pltpu.VMEM((2, bm, D_MODEL), F32),
                        pltpu.SemaphoreType.DMA(()),
                        pltpu.SemaphoreType.DMA(())],
    )
    return pl.pallas_call(
        functools.partial(_moe_kernel, bm=bm, n_chunk=n_chunk, n_blocks=n_blocks),
        grid_spec=grid_spec,
        out_shape=jax.ShapeDtypeStruct((n_rows_out, D_MODEL), F32),
        compiler_params=_cparams(2),
        name="moe_experts",
    )(blk_expert, blk_valid, slot_tok.reshape(n_blocks, 1, bm), slot_tok.reshape(n_blocks, 1, bm),
      slot_dest.reshape(n_blocks, 1, bm), xn, wg, wu, wd, slot_wb)


def _final_kernel(x_ref, y0_ref, y1_ref, g_ref, o_ref):
    x = x_ref[...] + (y0_ref[...] + y1_ref[...])
    o_ref[...] = _rms(x, g_ref[...])


def _final(x, y, g, row0, rows, t, bm):
    rb, tb = row0 // bm, t // bm
    return pl.pallas_call(
        _final_kernel,
        grid=(rows // bm,),
        in_specs=[pl.BlockSpec((bm, D_MODEL), lambda i: (rb + i, 0)),
                  pl.BlockSpec((bm, D_MODEL), lambda i: (rb + i, 0)),
                  pl.BlockSpec((bm, D_MODEL), lambda i: (tb + rb + i, 0)),
                  pl.BlockSpec((1, D_MODEL), lambda i: (0, 0))],
        out_specs=pl.BlockSpec((bm, D_MODEL), lambda i: (i, 0)),
        out_shape=jax.ShapeDtypeStruct((rows, D_MODEL), F32),
        compiler_params=_cparams(1),
        name="final_norm",
    )(x, y, y, g)


def _rope_tables(n_pos):
    pos = jnp.arange(n_pos, dtype=F32)
    ones = jnp.ones((n_pos, 1), F32)
    zeros = jnp.zeros((n_pos, 1), F32)

    def angles(p, dims, theta):
        inv = jnp.power(jnp.float32(theta), -jnp.arange(0, dims, 2, dtype=F32) / dims)
        return p[:, None] * inv[None, :]

    ang = angles(pos, ROPE_DIMS, ROPE_THETA)
    rest = HEAD_DIM - ROPE_DIMS
    cos_a = jnp.concatenate([jnp.cos(ang), jnp.cos(ang), jnp.tile(ones, (1, rest))], axis=1)
    half = jnp.tile(zeros, (1, ROPE_DIMS // 2))
    tail = jnp.tile(zeros, (1, rest))
    lo_a = jnp.concatenate([-jnp.sin(ang), half, tail], axis=1)
    hi_a = jnp.concatenate([half, jnp.sin(ang), tail], axis=1)
    row = jnp.floor(pos / GRID_W)
    col = pos - row * GRID_W
    ar = angles(row, HEAD_DIM // 2, AXIAL_THETA)
    ac = angles(col, HEAD_DIM // 2, AXIAL_THETA)
    quarter = jnp.tile(zeros, (1, HEAD_DIM // 4))
    cos_b = jnp.concatenate([jnp.cos(ar), jnp.cos(ar), jnp.cos(ac), jnp.cos(ac)], axis=1)
    lo_b = jnp.concatenate([-jnp.sin(ar), quarter, -jnp.sin(ac), quarter], axis=1)
    hi_b = jnp.concatenate([quarter, jnp.sin(ar), quarter, jnp.sin(ac)], axis=1)
    return (cos_a, lo_a, hi_a), (cos_b, lo_b, hi_b)


def _routing_slots(route, t, bm):
    n_asg = t * TOP_K
    flat_e = route[:, :TOP_K].astype(jnp.int32).reshape(-1)
    flat_w = route[:, TOP_K:2 * TOP_K].reshape(-1)
    asg = jnp.arange(n_asg, dtype=jnp.int32)
    _, asg_sorted, w_sorted = lax.sort((flat_e, asg, lax.bitcast_convert_type(flat_w, jnp.int32)),
                                       num_keys=1, is_stable=True)
    experts = jnp.arange(N_EXPERTS, dtype=jnp.int32)
    counts = jnp.sum((flat_e[:, None] == experts[None, :]).astype(jnp.int32), axis=0)
    start = jnp.cumsum(counts) - counts
    padded = (counts + bm - 1) // bm * bm
    padded_end = jnp.cumsum(padded)
    padded_start = padded_end - padded
    n_slots = -(-(n_asg + N_EXPERTS * (bm - 1)) // bm) * bm
    n_blocks = n_slots // bm
    blk_start = jnp.arange(n_blocks, dtype=jnp.int32) * bm
    blk_expert = jnp.minimum(jnp.searchsorted(padded_end, blk_start, side='right'),
                             N_EXPERTS - 1).astype(jnp.int32)
    blk_valid = jnp.clip(padded_start[blk_expert] + counts[blk_expert] - blk_start, 0, bm)
    blk_valid = jnp.where(blk_start < padded_end[-1], blk_valid, 0).astype(jnp.int32)
    slot = jnp.arange(n_slots, dtype=jnp.int32)
    slot_e = jnp.repeat(blk_expert, bm)
    rank = slot - padded_start[slot_e]
    valid = jnp.logical_and(rank < counts[slot_e], slot < padded_end[-1])
    src = jnp.clip(start[slot_e] + rank, 0, n_asg - 1)
    picked = jnp.stack([asg_sorted, w_sorted], axis=1)[src]
    a = picked[:, 0]
    tok = a // TOP_K
    pads_before = slot - start[slot_e] - jnp.minimum(rank, counts[slot_e])
    slot_tok = jnp.where(valid, tok, 0)
    slot_dest = jnp.where(valid, (a % TOP_K) * t + tok, n_asg + pads_before)
    slot_w = jnp.where(valid, lax.bitcast_convert_type(picked[:, 1], F32), 0.0)
    slot_wb = jnp.broadcast_to(slot_w[:, None], (n_slots, LANES))
    return blk_expert, slot_tok, slot_dest, blk_valid, slot_wb


def _tile(n, pref):
    while n % pref:
        pref //= 2
    return pref


def _trunk(x_prompt, x_sample, attn_norm, w_in, b_gate, q_norm, k_norm, w_proj_a, w_proj_b,
           w_out, ffn_norm, ffn_w_gate, ffn_w_up, ffn_w_down, moe_router, moe_w_gate,
           moe_w_up, moe_w_down, final_norm):
    bp, sp, _ = x_prompt.shape
    bs, ss, _ = x_sample.shape
    tp, ts = bp * sp, bs * ss
    t = tp + ts
    depth = w_in.shape[0]
    assert depth == 2, "dense-FFN layer followed by a final MoE layer"
    assert ss % sp == 0 and tp % ss == 0, "blocks of ss tokens must hold whole sequences"
    sets = ((0, tp), (tp, ts))

    seq_unit = math.gcd(sp, ss)
    bm = _tile(seq_unit, 1024)
    bm_small = _tile(seq_unit, 512)
    n_p_blk, n_pb, n_sb = tp // bm, sp // bm, ss // bm

    def pos_blk(i, j):
        return (jnp.where(i < n_p_blk, i % n_pb, (i - n_p_blk) % n_sb), 0)

    tab_a, tab_b = _rope_tables(max(sp, ss))

    row2 = lambda v: v.reshape(1, -1).astype(F32)
    x, xn = _norm_in(x_prompt.reshape(tp, D_MODEL), x_sample.reshape(ts, D_MODEL),
                     row2(attn_norm[0]), bm_small)

    out = None
    for layer in range(depth):
        qk_a, v_a, q_b, k_b, v_b, gates = _proj_in(
            xn, w_in[layer].astype(BF16), tab_a, tab_b, row2(q_norm[layer]), row2(k_norm[layer]),
            row2(b_gate[layer]), pos_blk, bm)

        oa, lse = [], []
        for g in range(A_N_GROUPS):
            o_g, lse_g = _attn_a(qk_a, v_a, g, tp, sp, ss)
            oa.append(o_g)
            lse.append(lse_g)
        ob = _attn_b(q_b, k_b, v_b, tp, sp, ss)

        merged = _merge(oa, lse, ob, gates, w_proj_a[layer].astype(BF16),
                        w_proj_b[layer].astype(BF16), bm_small // 2)
        x, xn_ffn = _matmul(
            merged, w_out[layer].astype(BF16), bm=bm_small, bn=D_MODEL,
            epilogue=_epi_residual_norm,
            extras=[x, row2(ffn_norm[layer])],
            extra_specs=[pl.BlockSpec((bm_small, D_MODEL), lambda i, j: (i, 0)),
                         pl.BlockSpec((1, D_MODEL), lambda i, j: (0, 0))],
            out_shape=[jax.ShapeDtypeStruct((t, D_MODEL), F32),
                       jax.ShapeDtypeStruct((t, D_MODEL), BF16)],
            out_specs=[pl.BlockSpec((bm_small, D_MODEL), lambda i, j: (i, 0))] * 2,
            name="proj_out")

        j = layer // 2
        if layer % 2 == 0:
            x, xn = _ffn(xn_ffn, ffn_w_gate[j].astype(BF16), ffn_w_up[j].astype(BF16),
                         ffn_w_down[j].astype(BF16), x, row2(attn_norm[layer + 1]), bm_small, 512)
        else:
            wr = jnp.pad(moe_router[j].astype(F32), ((0, 0), (0, LANES - N_EXPERTS)))
            xn_moe, route = _router(x, row2(ffn_norm[layer]), wr, bm_small)
            meta = _routing_slots(route, t, MOE_BM)
            n_rows_out = meta[1].shape[0]
            y = _moe(xn_moe, moe_w_gate[j].astype(BF16), moe_w_up[j].astype(BF16),
                     moe_w_down[j].astype(BF16), *meta, n_rows_out, MOE_BM, MOE_BC)
            out = tuple(_final(x, y, row2(final_norm), row0, rows, t, bm_small)
                        for row0, rows in sets)
    y_prompt, y_sample = out
    return y_prompt.reshape(bp, sp, D_MODEL), y_sample.reshape(bs, ss, D_MODEL)


def kernel(x_prompt, x_sample, attn_norm, w_in, b_gate, q_norm, k_norm, w_proj_a, w_proj_b,
           w_out, ffn_norm, ffn_w_gate, ffn_w_up, ffn_w_down, moe_router, moe_w_gate,
           moe_w_up, moe_w_down, final_norm):
    return _trunk(x_prompt, x_sample, attn_norm, w_in, b_gate, q_norm, k_norm, w_proj_a,
                  w_proj_b, w_out, ffn_norm, ffn_w_gate, ffn_w_up, ffn_w_down, moe_router,
                  moe_w_gate, moe_w_up, moe_w_down, final_norm)
```

```python
import functools
import math

import jax
import jax.numpy as jnp
from jax import lax
from jax.experimental import pallas as pl
from jax.experimental.pallas import tpu as pltpu

F32 = jnp.float32
BF16 = jnp.bfloat16

D_MODEL = 2048
HEAD_DIM = 128
ATTN_SCALE = HEAD_DIM ** -0.5
EPS = 1e-6
NEG = -1e30

A_WINDOWS = (128, 512, 2048)
A_DILATIONS = (1, 4, 16)
A_N_GROUPS = 3
A_HEADS = 4
A_WIDTH = A_HEADS * HEAD_DIM
A_QK = 2 * A_N_GROUPS * A_WIDTH
A_QKV = 3 * A_N_GROUPS * A_WIDTH
A_RADIUS = 64
ROPE_THETA = 500000.0
ROPE_DIMS = HEAD_DIM // 4

B_Q_HEADS = 8
B_KV_HEADS = 2
B_GROUP = B_Q_HEADS // B_KV_HEADS
B_WIDTH = B_Q_HEADS * HEAD_DIM
B_KVW = B_KV_HEADS * HEAD_DIM
B_QKV = B_WIDTH + 2 * B_KVW
AXIAL_THETA = 10000.0
GRID_W = 64

C_GATE = 2 * D_MODEL
N_EXPERTS = 8
TOP_K = 2

V7X_VMEM_BYTES = 64 * 1024 * 1024
VMEM_LIMIT = V7X_VMEM_BYTES - 8 * 1024 * 1024
LANES = 128

LOG2E = math.log2(math.e)


def _cparams(n_axes, vmem=VMEM_LIMIT):
    return pltpu.CompilerParams(dimension_semantics=("arbitrary",) * n_axes,
                                vmem_limit_bytes=vmem)


def _rms(x, g):
    return x * lax.rsqrt(jnp.mean(x * x, axis=-1, keepdims=True) + EPS) * g


def _rope128(x, c, s_lo, s_hi, w):
    return x * c + pltpu.roll(x, LANES - w, 1) * s_lo + pltpu.roll(x, w, 1) * s_hi


def _norm_in_kernel(xp_ref, xs_ref, g_ref, x_ref, xn_ref, *, n_p):
    i = pl.program_id(0)

    def emit(src):
        x = src[...]
        x_ref[...] = x
        xn_ref[...] = _rms(x, g_ref[...]).astype(BF16)

    @pl.when(i < n_p)
    def _():
        emit(xp_ref)

    @pl.when(i >= n_p)
    def _():
        emit(xs_ref)


def _norm_in(xp, xs, g, bm):
    tp, ts = xp.shape[0], xs.shape[0]
    n_p, n_s = tp // bm, ts // bm
    t = tp + ts
    return pl.pallas_call(
        functools.partial(_norm_in_kernel, n_p=n_p),
        grid=(n_p + n_s,),
        in_specs=[pl.BlockSpec((bm, D_MODEL), lambda i: (jnp.minimum(i, n_p - 1), 0)),
                  pl.BlockSpec((bm, D_MODEL), lambda i: (jnp.maximum(i - n_p, 0), 0)),
                  pl.BlockSpec((1, D_MODEL), lambda i: (0, 0))],
        out_specs=[pl.BlockSpec((bm, D_MODEL), lambda i: (i, 0)),
                   pl.BlockSpec((bm, D_MODEL), lambda i: (i, 0))],
        out_shape=[jax.ShapeDtypeStruct((t, D_MODEL), F32),
                   jax.ShapeDtypeStruct((t, D_MODEL), BF16)],
        compiler_params=_cparams(1),
        name="norm_in",
    )(xp, xs, g)


def _mm_kernel(a_ref, w_ref, *refs, epilogue, n_extra):
    acc = jnp.dot(a_ref[...], w_ref[...], preferred_element_type=F32)
    epilogue(acc, refs[:n_extra], refs[n_extra:])


def _matmul(a, w, *, bm, bn, epilogue, extras=(), extra_specs=(), out_shape, out_specs, name):
    m, k = a.shape
    n = w.shape[1]
    assert m % bm == 0 and n % bn == 0, (m, n, bm, bn)
    return pl.pallas_call(
        functools.partial(_mm_kernel, epilogue=epilogue, n_extra=len(extras)),
        grid=(m // bm, n // bn),
        in_specs=[pl.BlockSpec((bm, k), lambda i, j: (i, 0)),
                  pl.BlockSpec((k, bn), lambda i, j: (0, j)),
                  *extra_specs],
        out_specs=out_specs,
        out_shape=out_shape,
        compiler_params=_cparams(2),
        name=name,
    )(a, w, *extras)


def _epi_residual_norm(acc, extras, outs):
    x = extras[0][...] + acc
    outs[0][...] = x
    outs[1][...] = _rms(x, extras[1][...]).astype(outs[1].dtype)


PROJ_BN = 2 * B_KVW
_J_VA = A_QK // PROJ_BN
_J_QB = A_QKV // PROJ_BN
_J_KV = (A_QKV + B_WIDTH) // PROJ_BN
_J_GATE = (A_QKV + B_QKV) // PROJ_BN
_J_END = (A_QKV + B_QKV + C_GATE) // PROJ_BN


def _proj_in_kernel(xn_ref, w_ref, ca_ref, la_ref, ha_ref, cb_ref, lb_ref, hb_ref,
                    gq_ref, gk_ref, bias_ref,
                    qka_ref, va_ref, qb_ref, kb_ref, vb_ref, gate_ref, acc_ref):
    j = pl.program_id(1)
    p = j - 1

    def matmul():
        acc_ref[...] = jnp.dot(xn_ref[...], w_ref[...], preferred_element_type=F32)

    def heads(n):
        return [slice(h * LANES, (h + 1) * LANES) for h in range(n // LANES)]

    def norm_rope_b(x, g_ref):
        return _rope128(_rms(x, g_ref[...]), cb_ref[...], lb_ref[...], hb_ref[...], HEAD_DIM // 4)

    def epi_qk_a():
        for sl in heads(PROJ_BN):
            qka_ref[:, sl] = _rope128(acc_ref[:, sl], ca_ref[...], la_ref[...], ha_ref[...],
                                      ROPE_DIMS // 2)

    def epi_v_a():
        va_ref[...] = acc_ref[...]

    def epi_q_b():
        tq = qb_ref.shape[2]
        for h, sl in enumerate(heads(PROJ_BN)):
            yt = (norm_rope_b(acc_ref[:, sl], gq_ref) * (ATTN_SCALE * LOG2E)).T
            for n in range(qb_ref.shape[0]):
                qb_ref[n, h * HEAD_DIM:(h + 1) * HEAD_DIM, :] = (
                    yt[:, n * tq:(n + 1) * tq].astype(BF16))

    def epi_kv_b():
        for sl in heads(B_KVW):
            kb_ref[:, sl] = norm_rope_b(acc_ref[:, sl], gk_ref).astype(BF16)
        tk = vb_ref.shape[2]
        vt = acc_ref[:, B_KVW:].T
        for n in range(vb_ref.shape[0]):
            vb_ref[n] = vt[:, n * tk:(n + 1) * tk].astype(BF16)

    def epi_gate():
        gate_ref[...] = jax.nn.sigmoid(acc_ref[...] + bias_ref[...]).astype(BF16)

    def between(lo, hi):
        return jnp.logical_and(p >= lo, p < hi)

    @pl.when(j == 0)
    def _():
        matmul()

    for lo, hi, epilogue in ((0, _J_VA, epi_qk_a), (_J_VA, _J_QB, epi_v_a), (_J_QB, _J_KV, epi_q_b),
                             (_J_KV, _J_GATE, epi_kv_b), (_J_GATE, _J_END - 1, epi_gate)):
        @pl.when(between(lo, hi))
        def _(epilogue=epilogue):
            epilogue()
            matmul()

    @pl.when(p == _J_END - 1)
    def _():
        epi_gate()


def _proj_in(xn, w, tab_a, tab_b, gq, gk, bias, pos_blk, bm):
    t = xn.shape[0]
    bn = PROJ_BN
    tq, tk = ATTN_B_TQ, ATTN_B_TK
    assert bm % tq == 0 and bm % tk == 0

    def cols(lo, hi):
        return lambda i, j: (i, jnp.clip(j - 1 - lo, 0, hi - lo - 1))

    tab_spec = pl.BlockSpec((bm, LANES), pos_blk)
    vec_spec = pl.BlockSpec((1, LANES), lambda i, j: (0, 0))
    return pl.pallas_call(
        _proj_in_kernel,
        grid=(t // bm, _J_END + 1),
        in_specs=[pl.BlockSpec((bm, D_MODEL), lambda i, j: (i, 0)),
                  pl.BlockSpec((D_MODEL, bn), lambda i, j: (0, jnp.minimum(j, _J_END - 1))),
                  *([tab_spec] * 6), vec_spec, vec_spec,
                  pl.BlockSpec((1, bn), lambda i, j: (0, jnp.clip(j - 1 - _J_GATE, 0, _J_END - _J_GATE - 1)))],
        out_specs=[pl.BlockSpec((bm, bn), cols(0, _J_VA)),
                   pl.BlockSpec((bm, bn), cols(_J_VA, _J_QB)),
                   pl.BlockSpec((bm // tq, bn, tq),
                                lambda i, j: (i, jnp.clip(j - 1 - _J_QB, 0, _J_KV - _J_QB - 1), 0)),
                   pl.BlockSpec((bm, B_KVW), lambda i, j: (i, 0)),
                   pl.BlockSpec((bm // tk, B_KVW, tk), lambda i, j: (i, 0, 0)),
                   pl.BlockSpec((bm, bn), cols(_J_GATE, _J_END))],
        out_shape=[jax.ShapeDtypeStruct((t, A_QK), F32),
                   jax.ShapeDtypeStruct((t, A_QKV - A_QK), F32),
                   jax.ShapeDtypeStruct((t // tq, B_WIDTH, tq), BF16),
                   jax.ShapeDtypeStruct((t, B_KVW), BF16),
                   jax.ShapeDtypeStruct((t // tk, B_KVW, tk), BF16),
                   jax.ShapeDtypeStruct((t, C_GATE), BF16)],
        scratch_shapes=[pltpu.VMEM((bm, bn), F32)],
        compiler_params=_cparams(2),
        name="proj_in",
    )(xn, w, *tab_a, *tab_b, gq, gk, bias)


ATTN_A_UNROLL = 4
ATTN_A_TQ = 256
ATTN_A_WHOLE = 512


def _attn_a_tiles(l_sub):
    if l_sub <= ATTN_A_WHOLE:
        return l_sub, l_sub
    assert l_sub % ATTN_A_TQ == 0
    return ATTN_A_TQ, ATTN_A_TQ + 2 * A_RADIUS


def _band_bias(tq, win):
    r = jnp.arange(tq, dtype=jnp.int32)[:, None]
    c = jnp.arange(win, dtype=jnp.int32)[None, :]
    offs = (0, A_RADIUS, win - tq)
    return jnp.stack([jnp.where(jnp.abs(c - r - off) <= A_RADIUS, 0.0, NEG).astype(F32)
                      for off in offs])


def _attn_a_kernel(q_ref, k_ref, v_ref, bias_p_ref, bias_s_ref, o_ref, lse_ref, *,
                   d, n_short_blk, sp, ss):
    def rows(start, size):
        if d == 1:
            return pl.ds(start, size)
        return pl.ds(start, size, stride=d)

    def run(seq, n_seq, bias_ref):
        l_sub = seq // d
        tq, win = _attn_a_tiles(l_sub)
        n_tile = l_sub // tq

        def tile(idx, carry):
            base = (idx // (d * n_tile)) * seq
            r = (idx // n_tile) % d
            ti = idx % n_tile
            t0 = ti * tq
            ks = jnp.clip(t0 - A_RADIUS, 0, l_sub - win)
            if n_tile == 1:
                bias = bias_ref[0]
            else:
                bias = bias_ref[jnp.where(ti == 0, 0, jnp.where(ti == n_tile - 1, 2, 1))]
            q = (q_ref[rows(base + r + t0 * d, tq), :] * ATTN_SCALE).astype(BF16)
            k = k_ref[rows(base + r + ks * d, win), :].astype(BF16)
            v = v_ref[rows(base + r + ks * d, win), :].astype(BF16)
            s = lax.dot_general(q, k, (((1,), (1,)), ((), ())), preferred_element_type=F32) + bias
            m = jnp.max(s, axis=-1, keepdims=True)
            p = jnp.exp(s - m)
            den = jnp.sum(p, axis=-1, keepdims=True)
            o = jnp.dot(p.astype(BF16), v, preferred_element_type=F32) / den
            lse = m + jnp.log(den)
            o_ref[rows(base + r + t0 * d, tq), :] = o
            lse_ref[rows(base + r + t0 * d, tq), :] = jnp.broadcast_to(lse, (tq, LANES))
            return carry

        lax.fori_loop(0, n_seq * d * n_tile, tile, 0, unroll=ATTN_A_UNROLL)

    if sp == ss:
        run(ss, 1, bias_s_ref)
    else:
        blk = pl.program_id(0)

        @pl.when(blk < n_short_blk)
        def _():
            run(sp, ss // sp, bias_p_ref)

        @pl.when(blk >= n_short_blk)
        def _():
            run(ss, 1, bias_s_ref)


def _attn_a(qk, v, g, tp, sp, ss):
    t = qk.shape[0]
    d = A_DILATIONS[g]
    biases = [_band_bias(*_attn_a_tiles(seq // d)) for seq in (sp, ss)]
    kern = functools.partial(_attn_a_kernel, d=d, n_short_blk=tp // ss, sp=sp, ss=ss)
    return pl.pallas_call(
        kern,
        grid=(t // ss, A_HEADS),
        in_specs=[pl.BlockSpec((ss, LANES), lambda b, h: (b, g * A_HEADS + h)),
                  pl.BlockSpec((ss, LANES), lambda b, h: (b, (A_N_GROUPS + g) * A_HEADS + h)),
                  pl.BlockSpec((ss, LANES), lambda b, h: (b, g * A_HEADS + h)),
                  *[pl.BlockSpec(bias.shape, lambda b, h: (0, 0, 0)) for bias in biases]],
        out_specs=[pl.BlockSpec((ss, LANES), lambda b, h: (b, h))] * 2,
        out_shape=[jax.ShapeDtypeStruct((t, A_WIDTH), F32)] * 2,
        compiler_params=_cparams(2),
        name=f"attn_a{g}",
    )(qk, qk, v, *biases)


ATTN_B_TQ = 512
ATTN_B_TK = 1024


def _attn_b_kernel(qt_ref, k_ref, vt_ref, o_ref, s_ref, *, tq, tk, tp, sp, ss):
    row = pl.program_id(1) * tq
    short = row < tp
    first = jnp.where(short, ((row % ss) // sp) * sp, 0) // tk
    n_chunk = jnp.where(short, sp // tk, ss // tk)
    last = first + n_chunk - 1
    qt = jnp.concatenate([qt_ref[h * HEAD_DIM:(h + 1) * HEAD_DIM, :] for h in range(B_GROUP)],
                         axis=1)
    cols = B_GROUP * tq

    def scores(c, slot):
        k = k_ref[pl.ds(pl.multiple_of(c * tk, tk), tk), :]
        s_ref[slot] = jnp.dot(k, qt, preferred_element_type=F32)

    def chunk(c, slot, carry, prefetch=True):
        if prefetch:
            scores(c + 1, 1 - slot)
        m, l, acc = carry
        st = s_ref[slot]
        m_new = jnp.maximum(m, jnp.max(st, axis=0, keepdims=True))
        alpha = jnp.exp2(m - m_new)
        pt = jnp.exp2(st - m_new)
        l = alpha * l + jnp.sum(pt, axis=0, keepdims=True)
        acc = alpha * acc + jnp.dot(vt_ref[c], pt.astype(BF16), preferred_element_type=F32)
        return m_new, l, acc

    def body(it, carry):
        for slot in range(2):
            carry = chunk(first + 2 * it + slot, slot, carry)
        return carry

    scores(first, 0)
    init = (jnp.full((1, cols), NEG, F32), jnp.zeros((1, cols), F32),
            jnp.zeros((HEAD_DIM, cols), F32))
    carry = lax.fori_loop(0, n_chunk // 2 - 1, body, init)
    carry = chunk(last - 1, 0, carry)
    _, l, acc = chunk(last, 1, carry, prefetch=False)
    ot = acc / l
    for h in range(B_GROUP):
        o_ref[:, h * LANES:(h + 1) * LANES] = ot[:, h * tq:(h + 1) * tq].T.astype(o_ref.dtype)


def _attn_b(qt, k, vt, tp, sp, ss):
    tq, tk = ATTN_B_TQ, ATTN_B_TK
    t = k.shape[0]
    gw = B_GROUP * HEAD_DIM
    assert sp % (2 * tk) == 0 and ss % (2 * tk) == 0 and sp % tq == 0
    return pl.pallas_call(
        functools.partial(_attn_b_kernel, tq=tq, tk=tk, tp=tp, sp=sp, ss=ss),
        grid=(B_KV_HEADS, t // tq),
        scratch_shapes=[pltpu.VMEM((2, tk, gw // HEAD_DIM * tq), F32)],
        in_specs=[pl.BlockSpec((None, gw, tq), lambda kv, i: (i, kv, 0)),
                  pl.BlockSpec((ss, HEAD_DIM), lambda kv, i: (i // (ss // tq), kv)),
                  pl.BlockSpec((ss // tk, HEAD_DIM, tk), lambda kv, i: (i // (ss // tq), kv, 0))],
        out_specs=pl.BlockSpec((tq, gw), lambda kv, i: (i, kv)),
        out_shape=jax.ShapeDtypeStruct((t, B_WIDTH), BF16),
        compiler_params=_cparams(2),
        name="attn_b",
    )(qt, k, vt)


def _merge_kernel(o0, o1, o2, l0, l1, l2, ob_ref, gate_ref, wpa_ref, wpb_ref, out_ref):
    lse = [l0[...], l1[...], l2[...]]
    top = jnp.maximum(jnp.maximum(lse[0], lse[1]), lse[2])
    e = [jnp.exp(x - top) for x in lse]
    oa = (e[0] * o0[...] + e[1] * o1[...] + e[2] * o2[...]) / (e[0] + e[1] + e[2])
    pa = jnp.dot(oa.astype(BF16), wpa_ref[...], preferred_element_type=F32)
    pb = jnp.dot(ob_ref[...], wpb_ref[...], preferred_element_type=F32)
    ga = gate_ref[:, :D_MODEL].astype(F32)
    gb = gate_ref[:, D_MODEL:].astype(F32)
    out_ref[...] = (ga * pa + gb * pb).astype(out_ref.dtype)


def _merge(oa, lse, ob, gates, wpa, wpb, bm):
    t = ob.shape[0]
    row = lambda w: pl.BlockSpec((bm, w), lambda i: (i, 0))
    const = lambda s: pl.BlockSpec(s, lambda i: (0, 0))
    return pl.pallas_call(
        _merge_kernel,
        grid=(t // bm,),
        in_specs=[row(A_WIDTH)] * 6 + [row(B_WIDTH), row(C_GATE),
                                       const((A_WIDTH, D_MODEL)), const((B_WIDTH, D_MODEL))],
        out_specs=row(D_MODEL),
        out_shape=jax.ShapeDtypeStruct((t, D_MODEL), BF16),
        compiler_params=_cparams(1),
        name="merge",
    )(*oa, *lse, ob, gates, wpa, wpb)


def _ffn_kernel(xn_ref, wg_ref, wu_ref, wd_ref, x_ref, gn_ref, o_ref, on_ref, *, n_chunk):
    c = pl.program_id(1)

    @pl.when(c == 0)
    def _():
        o_ref[...] = x_ref[...]

    xn = xn_ref[...]
    g = jnp.dot(xn, wg_ref[...], preferred_element_type=F32)
    u = jnp.dot(xn, wu_ref[...], preferred_element_type=F32)
    hm = (g * jax.nn.sigmoid(g) * u).astype(BF16)
    o_ref[...] += jnp.dot(hm, wd_ref[...], preferred_element_type=F32)

    @pl.when(c == n_chunk - 1)
    def _():
        on_ref[...] = _rms(o_ref[...], gn_ref[...]).astype(on_ref.dtype)


def _ffn(xn, wg, wu, wd, x, g_next, bm, bc):
    t = xn.shape[0]
    f = wg.shape[1]
    n_chunk = f // bc
    row = lambda: pl.BlockSpec((bm, D_MODEL), lambda i, c: (i, 0))
    return pl.pallas_call(
        functools.partial(_ffn_kernel, n_chunk=n_chunk),
        grid=(t // bm, n_chunk),
        in_specs=[row(),
                  pl.BlockSpec((D_MODEL, bc), lambda i, c: (0, c)),
                  pl.BlockSpec((D_MODEL, bc), lambda i, c: (0, c)),
                  pl.BlockSpec((bc, D_MODEL), lambda i, c: (c, 0)),
                  row(),
                  pl.BlockSpec((1, D_MODEL), lambda i, c: (0, 0))],
        out_specs=[row(), row()],
        out_shape=[jax.ShapeDtypeStruct((t, D_MODEL), F32),
                   jax.ShapeDtypeStruct((t, D_MODEL), BF16)],
        compiler_params=_cparams(2),
        name="ffn_dense",
    )(xn, wg, wu, wd, x, g_next)


def _route(xn, wrt_ref):
    logits = [jnp.sum(xn * wrt_ref[e:e + 1, :], axis=-1, keepdims=True) for e in range(N_EXPERTS)]
    v1, i1 = logits[0], jnp.zeros(logits[0].shape, jnp.int32)
    for e in range(1, N_EXPERTS):
        better = logits[e] > v1
        i1 = jnp.where(better, e, i1)
        v1 = jnp.where(better, logits[e], v1)
    v2 = jnp.full(v1.shape, -jnp.inf, F32)
    i2 = jnp.zeros(i1.shape, jnp.int32)
    for e in range(N_EXPERTS):
        cand = jnp.where(i1 == e, -jnp.inf, logits[e])
        better = cand > v2
        i2 = jnp.where(better, e, i2)
        v2 = jnp.where(better, cand, v2)
    e2 = jnp.exp(v2 - v1)
    w1 = 1.0 / (1.0 + e2)
    w2 = e2 / (1.0 + e2)
    lane = lax.broadcasted_iota(jnp.int32, (xn.shape[0], LANES), 1)
    return jnp.where(lane == 0, i1.astype(F32),
                     jnp.where(lane == 1, i2.astype(F32),
                               jnp.where(lane == 2, w1, jnp.where(lane == 3, w2, 0.0))))


def _epi_residual_route(acc, extras, outs):
    x = extras[0][...] + acc
    xn = _rms(x, extras[1][...])
    outs[0][...] = x
    outs[1][...] = xn
    outs[2][...] = _route(xn, extras[2])


MOE_BM = 672
MOE_BC = 512


def _moe_kernel(be_ref, nvalid_ref,
                tok0_ref, tok_ref, dest_ref, xn_hbm, wg_ref, wu_ref, wd_ref, wb_ref, y_hbm,
                xs_f32, xs_bf, acc, sem_in, sem_out, *, bm, n_chunk, n_blocks):
    i = pl.program_id(0)
    c = pl.program_id(1)
    rows_per_step = bm // n_chunk
    is_block = i < n_blocks
    used = jnp.logical_and(is_block, nvalid_ref[jnp.minimum(i, n_blocks - 1)] > 0)
    gathers_next = i + 1 < n_blocks
    scatters_prev = i >= 1
    cur = i % 2

    def row_in(r, t):
        return pltpu.make_async_copy(xn_hbm.at[pl.ds(t, 1), :], xs_f32.at[pl.ds(r, 1), :], sem_in)

    def row_out(slot, r, t):
        return pltpu.make_async_copy(acc.at[slot, pl.ds(r, 1), :], y_hbm.at[pl.ds(t, 1), :], sem_out)

    def wait_all_in():
        pltpu.make_async_copy(xn_hbm.at[pl.ds(0, bm), :], xs_f32, sem_in).wait()

    def wait_all_out():
        pltpu.make_async_copy(acc.at[0], y_hbm.at[pl.ds(0, bm), :], sem_out).wait()

    def issue_gather():
        for k in range(rows_per_step):
            r = c * rows_per_step + k
            row_in(r, tok_ref[0, r]).start()

    def issue_scatter():
        for k in range(rows_per_step):
            r = c * rows_per_step + k
            row_out(1 - cur, r, dest_ref[0, r]).start()

    def compute():
        xs = xs_bf[...]
        g = jnp.dot(xs, wg_ref[...], preferred_element_type=F32)
        u = jnp.dot(xs, wu_ref[...], preferred_element_type=F32)
        hm = (g * jax.nn.sigmoid(g) * u).astype(BF16)
        part = jnp.dot(hm, wd_ref[...], preferred_element_type=F32)
        acc[cur] += part

    @pl.when(c == 0)
    def _():
        @pl.when(i == 0)
        def _():
            def start(r, carry):
                row_in(r, tok0_ref[0, r]).start()
                return carry
            lax.fori_loop(0, bm, start, 0)

        @pl.when(is_block)
        def _():
            wait_all_in()
            xs_bf[...] = xs_f32[...].astype(BF16)

        @pl.when(i >= 2)
        def _():
            wait_all_out()

        @pl.when(used)
        def _():
            acc[cur] = jnp.zeros((bm, D_MODEL), F32)

    fast = jnp.logical_and(jnp.logical_and(used, gathers_next), scatters_prev)

    @pl.when(fast)
    def _():
        issue_gather()
        issue_scatter()
        compute()

    @pl.when(jnp.logical_not(fast))
    def _():
        @pl.when(gathers_next)
        def _():
            issue_gather()

        @pl.when(scatters_prev)
        def _():
            issue_scatter()

        @pl.when(used)
        def _():
            compute()

    @pl.when(jnp.logical_and(used, c == n_chunk - 1))
    def _():
        acc[cur] = acc[cur] * jnp.tile(wb_ref[...], (1, D_MODEL // LANES))

    @pl.when(jnp.logical_and(i == n_blocks, c == n_chunk - 1))
    def _():
        wait_all_out()


def _moe(xn, wg, wu, wd, blk_expert, slot_tok, slot_dest, blk_valid, slot_wb, n_rows_out, bm, bc):
    n_slots = slot_tok.shape[0]
    n_blocks = n_slots // bm
    f = wg.shape[2]
    n_chunk = f // bc
    assert bm % n_chunk == 0 and n_blocks >= 2
    last = n_blocks - 1
    tok0_spec = pl.BlockSpec((None, 1, bm), lambda i, c, *_: (0, 0, 0), memory_space=pltpu.SMEM)
    tok_spec = pl.BlockSpec((None, 1, bm), lambda i, c, *_: (jnp.minimum(i + 1, last), 0, 0),
                            memory_space=pltpu.SMEM)
    dest_spec = pl.BlockSpec((None, 1, bm), lambda i, c, *_: (jnp.maximum(i - 1, 0), 0, 0),
                             memory_space=pltpu.SMEM)
    blk = lambda i: jnp.minimum(i, last)
    grid_spec = pltpu.PrefetchScalarGridSpec(
        num_scalar_prefetch=2,
        grid=(n_blocks + 1, n_chunk),
        in_specs=[tok0_spec, tok_spec, dest_spec,
                  pl.BlockSpec(memory_space=pl.ANY),
                  pl.BlockSpec((None, D_MODEL, bc), lambda i, c, be, *_: (be[blk(i)], 0, c)),
                  pl.BlockSpec((None, D_MODEL, bc), lambda i, c, be, *_: (be[blk(i)], 0, c)),
                  pl.BlockSpec((None, bc, D_MODEL), lambda i, c, be, *_: (be[blk(i)], c, 0)),
                  pl.BlockSpec((bm, LANES), lambda i, c, *_: (blk(i), 0))],
        out_specs=pl.BlockSpec(memory_space=pl.ANY),
        scratch_shapes=[pltpu.VMEM((bm, D_MODEL), F32),
                        pltpu.VMEM((bm, D_MODEL), BF16),
                        pltpu.VMEM((2, bm, D_MODEL), F32),
                        pltpu.SemaphoreType.DMA(()),
                        pltpu.SemaphoreType.DMA(())],
    )
    return pl.pallas_call(
        functools.partial(_moe_kernel, bm=bm, n_chunk=n_chunk, n_blocks=n_blocks),
        grid_spec=grid_spec,
        out_shape=jax.ShapeDtypeStruct((n_rows_out, D_MODEL), F32),
        compiler_params=_cparams(2),
        name="moe_experts",
    )(blk_expert, blk_valid, slot_tok.reshape(n_blocks, 1, bm), slot_tok.reshape(n_blocks, 1, bm),
      slot_dest.reshape(n_blocks, 1, bm), xn, wg, wu, wd, slot_wb)


def _final_kernel(x_ref, y0_ref, y1_ref, g_ref, o_ref):
    x = x_ref[...] + (y0_ref[...] + y1_ref[...])
    o_ref[...] = _rms(x, g_ref[...])


def _final(x, y, g, row0, rows, t, bm):
    rb, tb = row0 // bm, t // bm
    return pl.pallas_call(
        _final_kernel,
        grid=(rows // bm,),
        in_specs=[pl.BlockSpec((bm, D_MODEL), lambda i: (rb + i, 0)),
                  pl.BlockSpec((bm, D_MODEL), lambda i: (rb + i, 0)),
                  pl.BlockSpec((bm, D_MODEL), lambda i: (tb + rb + i, 0)),
                  pl.BlockSpec((1, D_MODEL), lambda i: (0, 0))],
        out_specs=pl.BlockSpec((bm, D_MODEL), lambda i: (i, 0)),
        out_shape=jax.ShapeDtypeStruct((rows, D_MODEL), F32),
        compiler_params=_cparams(1),
        name="final_norm",
    )(x, y, y, g)


def _rope_tables(n_pos):
    pos = jnp.arange(n_pos, dtype=F32)
    ones = jnp.ones((n_pos, 1), F32)
    zeros = jnp.zeros((n_pos, 1), F32)

    def angles(p, dims, theta):
        inv = jnp.power(jnp.float32(theta), -jnp.arange(0, dims, 2, dtype=F32) / dims)
        return p[:, None] * inv[None, :]

    ang = angles(pos, ROPE_DIMS, ROPE_THETA)
    rest = HEAD_DIM - ROPE_DIMS
    cos_a = jnp.concatenate([jnp.cos(ang), jnp.cos(ang), jnp.tile(ones, (1, rest))], axis=1)
    half = jnp.tile(zeros, (1, ROPE_DIMS // 2))
    tail = jnp.tile(zeros, (1, rest))
    lo_a = jnp.concatenate([-jnp.sin(ang), half, tail], axis=1)
    hi_a = jnp.concatenate([half, jnp.sin(ang), tail], axis=1)
    row = jnp.floor(pos / GRID_W)
    col = pos - row * GRID_W
    ar = angles(row, HEAD_DIM // 2, AXIAL_THETA)
    ac = angles(col, HEAD_DIM // 2, AXIAL_THETA)
    quarter = jnp.tile(zeros, (1, HEAD_DIM // 4))
    cos_b = jnp.concatenate([jnp.cos(ar), jnp.cos(ar), jnp.cos(ac), jnp.cos(ac)], axis=1)
    lo_b = jnp.concatenate([-jnp.sin(ar), quarter, -jnp.sin(ac), quarter], axis=1)
    hi_b = jnp.concatenate([quarter, jnp.sin(ar), quarter, jnp.sin(ac)], axis=1)
    return (cos_a, lo_a, hi_a), (cos_b, lo_b, hi_b)


def _routing_slots(route, t, bm):
    n_asg = t * TOP_K
    flat_e = route[:, :TOP_K].astype(jnp.int32).reshape(-1)
    flat_w = route[:, TOP_K:2 * TOP_K].reshape(-1)
    asg = jnp.arange(n_asg, dtype=jnp.int32)
    _, asg_sorted, w_sorted = lax.sort((flat_e, asg, lax.bitcast_convert_type(flat_w, jnp.int32)),
                                       num_keys=1, is_stable=True)
    experts = jnp.arange(N_EXPERTS, dtype=jnp.int32)
    counts = jnp.sum((flat_e[:, None] == experts[None, :]).astype(jnp.int32), axis=0)
    start = jnp.cumsum(counts) - counts
    padded = (counts + bm - 1) // bm * bm
    padded_end = jnp.cumsum(padded)
    padded_start = padded_end - padded
    n_slots = -(-(n_asg + N_EXPERTS * (bm - 1)) // bm) * bm
    n_blocks = n_slots // bm
    blk_start = jnp.arange(n_blocks, dtype=jnp.int32) * bm
    blk_expert = jnp.minimum(jnp.searchsorted(padded_end, blk_start, side='right'),
                             N_EXPERTS - 1).astype(jnp.int32)
    blk_valid = jnp.clip(padded_start[blk_expert] + counts[blk_expert] - blk_start, 0, bm)
    blk_valid = jnp.where(blk_start < padded_end[-1], blk_valid, 0).astype(jnp.int32)
    slot = jnp.arange(n_slots, dtype=jnp.int32)
    slot_e = jnp.repeat(blk_expert, bm)
    rank = slot - padded_start[slot_e]
    valid = jnp.logical_and(rank < counts[slot_e], slot < padded_end[-1])
    src = jnp.clip(start[slot_e] + rank, 0, n_asg - 1)
    picked = jnp.stack([asg_sorted, w_sorted], axis=1)[src]
    a = picked[:, 0]
    tok = a // TOP_K
    pads_before = slot - start[slot_e] - jnp.minimum(rank, counts[slot_e])
    slot_tok = jnp.where(valid, tok, 0)
    slot_dest = jnp.where(valid, (a % TOP_K) * t + tok, n_asg + pads_before)
    slot_w = jnp.where(valid, lax.bitcast_convert_type(picked[:, 1], F32), 0.0)
    slot_wb = jnp.broadcast_to(slot_w[:, None], (n_slots, LANES))
    return blk_expert, slot_tok, slot_dest, blk_valid, slot_wb


def _tile(n, pref):
    while n % pref:
        pref //= 2
    return pref


def _trunk(x_prompt, x_sample, attn_norm, w_in, b_gate, q_norm, k_norm, w_proj_a, w_proj_b,
           w_out, ffn_norm, ffn_w_gate, ffn_w_up, ffn_w_down, moe_router, moe_w_gate,
           moe_w_up, moe_w_down, final_norm):
    bp, sp, _ = x_prompt.shape
    bs, ss, _ = x_sample.shape
    tp, ts = bp * sp, bs * ss
    t = tp + ts
    depth = w_in.shape[0]
    assert depth == 2, "dense-FFN layer followed by a final MoE layer"
    assert ss % sp == 0 and tp % ss == 0, "blocks of ss tokens must hold whole sequences"
    sets = ((0, tp), (tp, ts))

    seq_unit = math.gcd(sp, ss)
    bm = _tile(seq_unit, 1024)
    bm_small = _tile(seq_unit, 512)
    n_p_blk, n_pb, n_sb = tp // bm, sp // bm, ss // bm

    def pos_blk(i, j):
        return (jnp.where(i < n_p_blk, i % n_pb, (i - n_p_blk) % n_sb), 0)

    tab_a, tab_b = _rope_tables(max(sp, ss))

    row2 = lambda v: v.reshape(1, -1).astype(F32)
    x, xn = _norm_in(x_prompt.reshape(tp, D_MODEL), x_sample.reshape(ts, D_MODEL),
                     row2(attn_norm[0]), bm_small)

    out = None
    for layer in range(depth):
        qk_a, v_a, q_b, k_b, v_b, gates = _proj_in(
            xn, w_in[layer].astype(BF16), tab_a, tab_b, row2(q_norm[layer]), row2(k_norm[layer]),
            row2(b_gate[layer]), pos_blk, bm)

        oa, lse = [], []
        for g in range(A_N_GROUPS):
            o_g, lse_g = _attn_a(qk_a, v_a, g, tp, sp, ss)
            oa.append(o_g)
            lse.append(lse_g)
        ob = _attn_b(q_b, k_b, v_b, tp, sp, ss)

        merged = _merge(oa, lse, ob, gates, w_proj_a[layer].astype(BF16),
                        w_proj_b[layer].astype(BF16), bm_small // 2)
        moe_layer = layer % 2 == 1
        row_spec = pl.BlockSpec((bm_small, D_MODEL), lambda i, j: (i, 0))
        vec_spec = pl.BlockSpec((1, D_MODEL), lambda i, j: (0, 0))
        if moe_layer:
            wr = moe_router[layer // 2].astype(F32).T
            x, xn_moe, route = _matmul(
                merged, w_out[layer].astype(BF16), bm=bm_small, bn=D_MODEL,
                epilogue=_epi_residual_route,
                extras=[x, row2(ffn_norm[layer]), wr],
                extra_specs=[row_spec, vec_spec, pl.BlockSpec((N_EXPERTS, D_MODEL), lambda i, j: (0, 0))],
                out_shape=[jax.ShapeDtypeStruct((t, D_MODEL), F32),
                           jax.ShapeDtypeStruct((t, D_MODEL), F32),
                           jax.ShapeDtypeStruct((t, LANES), F32)],
                out_specs=[row_spec, row_spec, pl.BlockSpec((bm_small, LANES), lambda i, j: (i, 0))],
                name="proj_out_route")
        else:
            x, xn_ffn = _matmul(
                merged, w_out[layer].astype(BF16), bm=bm_small, bn=D_MODEL,
                epilogue=_epi_residual_norm,
                extras=[x, row2(ffn_norm[layer])],
                extra_specs=[row_spec, vec_spec],
                out_shape=[jax.ShapeDtypeStruct((t, D_MODEL), F32),
                           jax.ShapeDtypeStruct((t, D_MODEL), BF16)],
                out_specs=[row_spec, row_spec],
                name="proj_out")

        j = layer // 2
        if layer % 2 == 0:
            x, xn = _ffn(xn_ffn, ffn_w_gate[j].astype(BF16), ffn_w_up[j].astype(BF16),
                         ffn_w_down[j].astype(BF16), x, row2(attn_norm[layer + 1]), bm_small, 512)
        else:
            meta = _routing_slots(route, t, MOE_BM)
            n_rows_out = meta[1].shape[0]
            y = _moe(xn_moe, moe_w_gate[j].astype(BF16), moe_w_up[j].astype(BF16),
                     moe_w_down[j].astype(BF16), *meta, n_rows_out, MOE_BM, MOE_BC)
            out = tuple(_final(x, y, row2(final_norm), row0, rows, t, bm_small)
                        for row0, rows in sets)
    y_prompt, y_sample = out
    return y_prompt.reshape(bp, sp, D_MODEL), y_sample.reshape(bs, ss, D_MODEL)


def kernel(x_prompt, x_sample, attn_norm, w_in, b_gate, q_norm, k_norm, w_proj_a, w_proj_b,
           w_out, ffn_norm, ffn_w_gate, ffn_w_up, ffn_w_down, moe_router, moe_w_gate,
           moe_w_up, moe_w_down, final_norm):
    return _trunk(x_prompt, x_sample, attn_norm, w_in, b_gate, q_norm, k_norm, w_proj_a,
                  w_proj_b, w_out, ffn_norm, ffn_w_gate, ffn_w_up, ffn_w_down, moe_router,
                  moe_w_gate, moe_w_up, moe_w_down, final_norm)
```

```python
import functools
import math

import jax
import jax.numpy as jnp
from jax import lax
from jax.experimental import pallas as pl
from jax.experimental.pallas import tpu as pltpu

F32 = jnp.float32
BF16 = jnp.bfloat16

D_MODEL = 2048
HEAD_DIM = 128
ATTN_SCALE = HEAD_DIM ** -0.5
EPS = 1e-6
NEG = -1e30

A_WINDOWS = (128, 512, 2048)
A_DILATIONS = (1, 4, 16)
A_N_GROUPS = 3
A_HEADS = 4
A_WIDTH = A_HEADS * HEAD_DIM
A_QK = 2 * A_N_GROUPS * A_WIDTH
A_QKV = 3 * A_N_GROUPS * A_WIDTH
A_RADIUS = 64
ROPE_THETA = 500000.0
ROPE_DIMS = HEAD_DIM // 4

B_Q_HEADS = 8
B_KV_HEADS = 2
B_GROUP = B_Q_HEADS // B_KV_HEADS
B_WIDTH = B_Q_HEADS * HEAD_DIM
B_KVW = B_KV_HEADS * HEAD_DIM
B_QKV = B_WIDTH + 2 * B_KVW
AXIAL_THETA = 10000.0
GRID_W = 64

C_GATE = 2 * D_MODEL
N_EXPERTS = 8
TOP_K = 2

V7X_VMEM_BYTES = 64 * 1024 * 1024
VMEM_LIMIT = V7X_VMEM_BYTES - 8 * 1024 * 1024
LANES = 128

LOG2E = math.log2(math.e)


def _cparams(n_axes, vmem=VMEM_LIMIT):
    return pltpu.CompilerParams(dimension_semantics=("arbitrary",) * n_axes,
                                vmem_limit_bytes=vmem)


def _rms(x, g):
    return x * lax.rsqrt(jnp.mean(x * x, axis=-1, keepdims=True) + EPS) * g


def _rope128(x, c, s_lo, s_hi, w):
    return x * c + pltpu.roll(x, LANES - w, 1) * s_lo + pltpu.roll(x, w, 1) * s_hi


def _norm_in_kernel(xp_ref, xs_ref, g_ref, x_ref, xn_ref, *, n_p):
    i = pl.program_id(0)

    def emit(src):
        x = src[...]
        x_ref[...] = x
        xn_ref[...] = _rms(x, g_ref[...]).astype(BF16)

    @pl.when(i < n_p)
    def _():
        emit(xp_ref)

    @pl.when(i >= n_p)
    def _():
        emit(xs_ref)


def _norm_in(xp, xs, g, bm):
    tp, ts = xp.shape[0], xs.shape[0]
    n_p, n_s = tp // bm, ts // bm
    t = tp + ts
    return pl.pallas_call(
        functools.partial(_norm_in_kernel, n_p=n_p),
        grid=(n_p + n_s,),
        in_specs=[pl.BlockSpec((bm, D_MODEL), lambda i: (jnp.minimum(i, n_p - 1), 0)),
                  pl.BlockSpec((bm, D_MODEL), lambda i: (jnp.maximum(i - n_p, 0), 0)),
                  pl.BlockSpec((1, D_MODEL), lambda i: (0, 0))],
        out_specs=[pl.BlockSpec((bm, D_MODEL), lambda i: (i, 0)),
                   pl.BlockSpec((bm, D_MODEL), lambda i: (i, 0))],
        out_shape=[jax.ShapeDtypeStruct((t, D_MODEL), F32),
                   jax.ShapeDtypeStruct((t, D_MODEL), BF16)],
        compiler_params=_cparams(1),
        name="norm_in",
    )(xp, xs, g)


def _mm_kernel(a_ref, w_ref, *refs, epilogue, n_extra):
    acc = jnp.dot(a_ref[...], w_ref[...], preferred_element_type=F32)
    epilogue(acc, refs[:n_extra], refs[n_extra:])


def _matmul(a, w, *, bm, bn, epilogue, extras=(), extra_specs=(), out_shape, out_specs, name):
    m, k = a.shape
    n = w.shape[1]
    assert m % bm == 0 and n % bn == 0, (m, n, bm, bn)
    return pl.pallas_call(
        functools.partial(_mm_kernel, epilogue=epilogue, n_extra=len(extras)),
        grid=(m // bm, n // bn),
        in_specs=[pl.BlockSpec((bm, k), lambda i, j: (i, 0)),
                  pl.BlockSpec((k, bn), lambda i, j: (0, j)),
                  *extra_specs],
        out_specs=out_specs,
        out_shape=out_shape,
        compiler_params=_cparams(2),
        name=name,
    )(a, w, *extras)


def _epi_residual_norm(acc, extras, outs):
    x = extras[0][...] + acc
    outs[0][...] = x
    outs[1][...] = _rms(x, extras[1][...]).astype(outs[1].dtype)


PROJ_BN = 2 * B_KVW
_J_VA = A_QK // PROJ_BN
_J_QB = A_QKV // PROJ_BN
_J_KV = (A_QKV + B_WIDTH) // PROJ_BN
_J_GATE = (A_QKV + B_QKV) // PROJ_BN
_J_END = (A_QKV + B_QKV + C_GATE) // PROJ_BN


def _proj_in_kernel(xn_ref, w_ref, ca_ref, la_ref, ha_ref, cb_ref, lb_ref, hb_ref,
                    gq_ref, gk_ref, bias_ref,
                    qka_ref, va_ref, qb_ref, kb_ref, vb_ref, gate_ref, acc_ref):
    j = pl.program_id(1)
    p = j - 1

    def matmul():
        acc_ref[...] = jnp.dot(xn_ref[...], w_ref[...], preferred_element_type=F32)

    def heads(n):
        return [slice(h * LANES, (h + 1) * LANES) for h in range(n // LANES)]

    def norm_rope_b(x, g_ref):
        return _rope128(_rms(x, g_ref[...]), cb_ref[...], lb_ref[...], hb_ref[...], HEAD_DIM // 4)

    def epi_qk_a():
        for sl in heads(PROJ_BN):
            qka_ref[:, sl] = _rope128(acc_ref[:, sl], ca_ref[...], la_ref[...], ha_ref[...],
                                      ROPE_DIMS // 2)

    def epi_v_a():
        va_ref[...] = acc_ref[...]

    def epi_q_b():
        tq = qb_ref.shape[2]
        for h, sl in enumerate(heads(PROJ_BN)):
            yt = (norm_rope_b(acc_ref[:, sl], gq_ref) * (ATTN_SCALE * LOG2E)).T
            for n in range(qb_ref.shape[0]):
                qb_ref[n, h * HEAD_DIM:(h + 1) * HEAD_DIM, :] = (
                    yt[:, n * tq:(n + 1) * tq].astype(BF16))

    def epi_kv_b():
        for sl in heads(B_KVW):
            kb_ref[:, sl] = norm_rope_b(acc_ref[:, sl], gk_ref).astype(BF16)
        tk = vb_ref.shape[2]
        vt = acc_ref[:, B_KVW:].T
        for n in range(vb_ref.shape[0]):
            vb_ref[n] = vt[:, n * tk:(n + 1) * tk].astype(BF16)

    def epi_gate():
        gate_ref[...] = (acc_ref[...] + bias_ref[...]).astype(BF16)

    def between(lo, hi):
        return jnp.logical_and(p >= lo, p < hi)

    @pl.when(j == 0)
    def _():
        matmul()

    for lo, hi, epilogue in ((0, _J_VA, epi_qk_a), (_J_VA, _J_QB, epi_v_a), (_J_QB, _J_KV, epi_q_b),
                             (_J_KV, _J_GATE, epi_kv_b), (_J_GATE, _J_END - 1, epi_gate)):
        @pl.when(between(lo, hi))
        def _(epilogue=epilogue):
            epilogue()
            matmul()

    @pl.when(p == _J_END - 1)
    def _():
        epi_gate()


def _proj_in(xn, w, tab_a, tab_b, gq, gk, bias, pos_blk, bm):
    t = xn.shape[0]
    bn = PROJ_BN
    tq, tk = ATTN_B_TQ, ATTN_B_TK
    assert bm % tq == 0 and bm % tk == 0

    def cols(lo, hi):
        return lambda i, j: (i, jnp.clip(j - 1 - lo, 0, hi - lo - 1))

    tab_spec = pl.BlockSpec((bm, LANES), pos_blk)
    vec_spec = pl.BlockSpec((1, LANES), lambda i, j: (0, 0))
    return pl.pallas_call(
        _proj_in_kernel,
        grid=(t // bm, _J_END + 1),
        in_specs=[pl.BlockSpec((bm, D_MODEL), lambda i, j: (i, 0)),
                  pl.BlockSpec((D_MODEL, bn), lambda i, j: (0, jnp.minimum(j, _J_END - 1))),
                  *([tab_spec] * 6), vec_spec, vec_spec,
                  pl.BlockSpec((1, bn), lambda i, j: (0, jnp.clip(j - 1 - _J_GATE, 0, _J_END - _J_GATE - 1)))],
        out_specs=[pl.BlockSpec((bm, bn), cols(0, _J_VA)),
                   pl.BlockSpec((bm, bn), cols(_J_VA, _J_QB)),
                   pl.BlockSpec((bm // tq, bn, tq),
                                lambda i, j: (i, jnp.clip(j - 1 - _J_QB, 0, _J_KV - _J_QB - 1), 0)),
                   pl.BlockSpec((bm, B_KVW), lambda i, j: (i, 0)),
                   pl.BlockSpec((bm // tk, B_KVW, tk), lambda i, j: (i, 0, 0)),
                   pl.BlockSpec((bm, bn), cols(_J_GATE, _J_END))],
        out_shape=[jax.ShapeDtypeStruct((t, A_QK), F32),
                   jax.ShapeDtypeStruct((t, A_QKV - A_QK), F32),
                   jax.ShapeDtypeStruct((t // tq, B_WIDTH, tq), BF16),
                   jax.ShapeDtypeStruct((t, B_KVW), BF16),
                   jax.ShapeDtypeStruct((t // tk, B_KVW, tk), BF16),
                   jax.ShapeDtypeStruct((t, C_GATE), BF16)],
        scratch_shapes=[pltpu.VMEM((bm, bn), F32)],
        compiler_params=_cparams(2),
        name="proj_in",
    )(xn, w, *tab_a, *tab_b, gq, gk, bias)


ATTN_A_UNROLL = 8
ATTN_A_TQ = 256
ATTN_A_WHOLE = 512


def _attn_a_tiles(l_sub):
    if l_sub <= ATTN_A_WHOLE:
        return l_sub, l_sub
    assert l_sub % ATTN_A_TQ == 0
    return ATTN_A_TQ, ATTN_A_TQ + 2 * A_RADIUS


def _band_bias(tq, win):
    r = jnp.arange(tq, dtype=jnp.int32)[:, None]
    c = jnp.arange(win, dtype=jnp.int32)[None, :]
    offs = (0, A_RADIUS, win - tq)
    return jnp.stack([jnp.where(jnp.abs(c - r - off) <= A_RADIUS, 0.0, NEG).astype(F32)
                      for off in offs])


def _attn_a_kernel(q_ref, k_ref, v_ref, bias_p_ref, bias_s_ref, o_ref, lse_ref, *,
                   d, n_short_blk, sp, ss):
    def rows(start, size):
        if d == 1:
            return pl.ds(start, size)
        return pl.ds(start, size, stride=d)

    def run(seq, n_seq, bias_ref):
        l_sub = seq // d
        tq, win = _attn_a_tiles(l_sub)
        n_tile = l_sub // tq

        def tile(idx, carry):
            base = (idx // (d * n_tile)) * seq
            r = (idx // n_tile) % d
            ti = idx % n_tile
            t0 = ti * tq
            ks = jnp.clip(t0 - A_RADIUS, 0, l_sub - win)
            if n_tile == 1:
                bias = bias_ref[0]
            else:
                bias = bias_ref[jnp.where(ti == 0, 0, jnp.where(ti == n_tile - 1, 2, 1))]
            q = (q_ref[rows(base + r + t0 * d, tq), :] * ATTN_SCALE).astype(BF16)
            k = k_ref[rows(base + r + ks * d, win), :].astype(BF16)
            v = v_ref[rows(base + r + ks * d, win), :].astype(BF16)
            s = lax.dot_general(q, k, (((1,), (1,)), ((), ())), preferred_element_type=F32) + bias
            m = jnp.max(s, axis=-1, keepdims=True)
            p = jnp.exp(s - m)
            den = jnp.sum(p, axis=-1, keepdims=True)
            o = jnp.dot(p.astype(BF16), v, preferred_element_type=F32) / den
            lse = m + jnp.log(den)
            o_ref[rows(base + r + t0 * d, tq), :] = o
            lse_ref[rows(base + r + t0 * d, tq), :] = jnp.broadcast_to(lse, (tq, LANES))
            return carry

        lax.fori_loop(0, n_seq * d * n_tile, tile, 0, unroll=ATTN_A_UNROLL)

    if sp == ss:
        run(ss, 1, bias_s_ref)
    else:
        blk = pl.program_id(0)

        @pl.when(blk < n_short_blk)
        def _():
            run(sp, ss // sp, bias_p_ref)

        @pl.when(blk >= n_short_blk)
        def _():
            run(ss, 1, bias_s_ref)


def _attn_a(qk, v, g, tp, sp, ss):
    t = qk.shape[0]
    d = A_DILATIONS[g]
    biases = [_band_bias(*_attn_a_tiles(seq // d)) for seq in (sp, ss)]
    kern = functools.partial(_attn_a_kernel, d=d, n_short_blk=tp // ss, sp=sp, ss=ss)
    return pl.pallas_call(
        kern,
        grid=(t // ss, A_HEADS),
        in_specs=[pl.BlockSpec((ss, LANES), lambda b, h: (b, g * A_HEADS + h)),
                  pl.BlockSpec((ss, LANES), lambda b, h: (b, (A_N_GROUPS + g) * A_HEADS + h)),
                  pl.BlockSpec((ss, LANES), lambda b, h: (b, g * A_HEADS + h)),
                  *[pl.BlockSpec(bias.shape, lambda b, h: (0, 0, 0)) for bias in biases]],
        out_specs=[pl.BlockSpec((ss, LANES), lambda b, h: (b, h))] * 2,
        out_shape=[jax.ShapeDtypeStruct((t, A_WIDTH), F32)] * 2,
        compiler_params=_cparams(2),
        name=f"attn_a{g}",
    )(qk, qk, v, *biases)


ATTN_B_TQ = 512
ATTN_B_TK = 1024


def _attn_b_kernel(qt_ref, qt_next_ref, k_ref, vt_ref, o_ref, s_ref, *, tq, tk, tp, sp, ss):
    i = pl.program_id(1)
    tiles_per_block = ss // tq

    def first_chunk(row):
        return jnp.where(row < tp, ((row % ss) // sp) * sp, 0) // tk

    row = i * tq
    first = first_chunk(row)
    n_chunk = jnp.where(row < tp, sp // tk, ss // tk)
    last = first + n_chunk - 1

    def heads_on_lanes(ref):
        return jnp.concatenate([ref[h * HEAD_DIM:(h + 1) * HEAD_DIM, :] for h in range(B_GROUP)],
                               axis=1)

    qt = heads_on_lanes(qt_ref)
    cols = B_GROUP * tq

    def scores(c, slot, q):
        k = k_ref[pl.ds(pl.multiple_of(c * tk, tk), tk), :]
        s_ref[slot] = jnp.dot(k, q, preferred_element_type=F32)

    def chunk(c, slot, carry, prefetch):
        prefetch()
        m, l, acc = carry
        st = s_ref[slot]
        m_new = jnp.maximum(m, jnp.max(st, axis=0, keepdims=True))
        alpha = jnp.exp2(m - m_new)
        pt = jnp.exp2(st - m_new)
        l = alpha * l + jnp.sum(pt, axis=0, keepdims=True)
        acc = alpha * acc + jnp.dot(vt_ref[c], pt.astype(BF16), preferred_element_type=F32)
        return m_new, l, acc

    def body(it, carry):
        for slot in range(2):
            c = first + 2 * it + slot
            carry = chunk(c, slot, carry, lambda c=c, slot=slot: scores(c + 1, 1 - slot, qt))
        return carry

    @pl.when(i % tiles_per_block == 0)
    def _():
        scores(first, 0, qt)

    init = (jnp.full((1, cols), NEG, F32), jnp.zeros((1, cols), F32),
            jnp.zeros((HEAD_DIM, cols), F32))
    carry = lax.fori_loop(0, n_chunk // 2 - 1, body, init)
    carry = chunk(last - 1, 0, carry, lambda: scores(last, 1, qt))
    next_first = first_chunk(row + tq)
    _, l, acc = chunk(last, 1, carry,
                      lambda: scores(next_first, 0, heads_on_lanes(qt_next_ref)))
    ot = acc / l
    for h in range(B_GROUP):
        o_ref[:, h * LANES:(h + 1) * LANES] = ot[:, h * tq:(h + 1) * tq].T.astype(o_ref.dtype)


def _attn_b(qt, k, vt, tp, sp, ss):
    tq, tk = ATTN_B_TQ, ATTN_B_TK
    t = k.shape[0]
    n_tile = t // tq
    gw = B_GROUP * HEAD_DIM
    assert sp % (2 * tk) == 0 and ss % (2 * tk) == 0 and sp % tq == 0
    return pl.pallas_call(
        functools.partial(_attn_b_kernel, tq=tq, tk=tk, tp=tp, sp=sp, ss=ss),
        grid=(B_KV_HEADS, n_tile),
        scratch_shapes=[pltpu.VMEM((2, tk, gw // HEAD_DIM * tq), F32)],
        in_specs=[pl.BlockSpec((None, gw, tq), lambda kv, i: (i, kv, 0)),
                  pl.BlockSpec((None, gw, tq), lambda kv, i: (jnp.minimum(i + 1, n_tile - 1), kv, 0)),
                  pl.BlockSpec((ss, HEAD_DIM), lambda kv, i: (i // (ss // tq), kv)),
                  pl.BlockSpec((ss // tk, HEAD_DIM, tk), lambda kv, i: (i // (ss // tq), kv, 0))],
        out_specs=pl.BlockSpec((tq, gw), lambda kv, i: (i, kv)),
        out_shape=jax.ShapeDtypeStruct((t, B_WIDTH), BF16),
        compiler_params=_cparams(2),
        name="attn_b",
    )(qt, qt, k, vt)


def _merge_kernel(o0, o1, o2, l0, l1, l2, ob_ref, gate_ref, wpa_ref, wpb_ref, out_ref):
    lse = [l0[...], l1[...], l2[...]]
    top = jnp.maximum(jnp.maximum(lse[0], lse[1]), lse[2])
    e = [jnp.exp(x - top) for x in lse]
    oa = (e[0] * o0[...] + e[1] * o1[...] + e[2] * o2[...]) / (e[0] + e[1] + e[2])
    pa = jnp.dot(oa.astype(BF16), wpa_ref[...], preferred_element_type=F32)
    pb = jnp.dot(ob_ref[...], wpb_ref[...], preferred_element_type=F32)
    ga = jax.nn.sigmoid(gate_ref[:, :D_MODEL].astype(F32))
    gb = jax.nn.sigmoid(gate_ref[:, D_MODEL:].astype(F32))
    out_ref[...] = (ga * pa + gb * pb).astype(out_ref.dtype)


def _merge(oa, lse, ob, gates, wpa, wpb, bm):
    t = ob.shape[0]
    row = lambda w: pl.BlockSpec((bm, w), lambda i: (i, 0))
    const = lambda s: pl.BlockSpec(s, lambda i: (0, 0))
    return pl.pallas_call(
        _merge_kernel,
        grid=(t // bm,),
        in_specs=[row(A_WIDTH)] * 6 + [row(B_WIDTH), row(C_GATE),
                                       const((A_WIDTH, D_MODEL)), const((B_WIDTH, D_MODEL))],
        out_specs=row(D_MODEL),
        out_shape=jax.ShapeDtypeStruct((t, D_MODEL), BF16),
        compiler_params=_cparams(1),
        name="merge",
    )(*oa, *lse, ob, gates, wpa, wpb)


def _ffn_kernel(xn_ref, wg_ref, wu_ref, wd_ref, x_ref, gn_ref, o_ref, on_ref, *, n_chunk):
    c = pl.program_id(1)

    @pl.when(c == 0)
    def _():
        o_ref[...] = x_ref[...]

    xn = xn_ref[...]
    g = jnp.dot(xn, wg_ref[...], preferred_element_type=F32)
    u = jnp.dot(xn, wu_ref[...], preferred_element_type=F32)
    hm = (g * jax.nn.sigmoid(g) * u).astype(BF16)
    o_ref[...] += jnp.dot(hm, wd_ref[...], preferred_element_type=F32)

    @pl.when(c == n_chunk - 1)
    def _():
        on_ref[...] = _rms(o_ref[...], gn_ref[...]).astype(on_ref.dtype)


def _ffn(xn, wg, wu, wd, x, g_next, bm, bc):
    t = xn.shape[0]
    f = wg.shape[1]
    n_chunk = f // bc
    row = lambda: pl.BlockSpec((bm, D_MODEL), lambda i, c: (i, 0))
    return pl.pallas_call(
        functools.partial(_ffn_kernel, n_chunk=n_chunk),
        grid=(t // bm, n_chunk),
        in_specs=[row(),
                  pl.BlockSpec((D_MODEL, bc), lambda i, c: (0, c)),
                  pl.BlockSpec((D_MODEL, bc), lambda i, c: (0, c)),
                  pl.BlockSpec((bc, D_MODEL), lambda i, c: (c, 0)),
                  row(),
                  pl.BlockSpec((1, D_MODEL), lambda i, c: (0, 0))],
        out_specs=[row(), row()],
        out_shape=[jax.ShapeDtypeStruct((t, D_MODEL), F32),
                   jax.ShapeDtypeStruct((t, D_MODEL), BF16)],
        compiler_params=_cparams(2),
        name="ffn_dense",
    )(xn, wg, wu, wd, x, g_next)


def _route(xn, wrt_ref):
    logits = [jnp.sum(xn * wrt_ref[e:e + 1, :], axis=-1, keepdims=True) for e in range(N_EXPERTS)]
    v1, i1 = logits[0], jnp.zeros(logits[0].shape, jnp.int32)
    for e in range(1, N_EXPERTS):
        better = logits[e] > v1
        i1 = jnp.where(better, e, i1)
        v1 = jnp.where(better, logits[e], v1)
    v2 = jnp.full(v1.shape, -jnp.inf, F32)
    i2 = jnp.zeros(i1.shape, jnp.int32)
    for e in range(N_EXPERTS):
        cand = jnp.where(i1 == e, -jnp.inf, logits[e])
        better = cand > v2
        i2 = jnp.where(better, e, i2)
        v2 = jnp.where(better, cand, v2)
    e2 = jnp.exp(v2 - v1)
    w1 = 1.0 / (1.0 + e2)
    w2 = e2 / (1.0 + e2)
    lane = lax.broadcasted_iota(jnp.int32, (xn.shape[0], LANES), 1)
    return jnp.where(lane == 0, i1.astype(F32),
                     jnp.where(lane == 1, i2.astype(F32),
                               jnp.where(lane == 2, w1, jnp.where(lane == 3, w2, 0.0))))


def _epi_residual_route(acc, extras, outs):
    x = extras[0][...] + acc
    xn = _rms(x, extras[1][...])
    outs[0][...] = x
    outs[1][...] = xn
    outs[2][...] = _route(xn, extras[2])


MOE_BM = 672
MOE_BC = 512


def _moe_kernel(be_ref, nvalid_ref,
                tok0_ref, tok_ref, dest_ref, xn_hbm, wg_ref, wu_ref, wd_ref, wb_ref, y_hbm,
                xs_f32, xs_bf, acc, sem_in, sem_out, *, bm, n_chunk, n_blocks):
    i = pl.program_id(0)
    c = pl.program_id(1)
    rows_per_step = bm // n_chunk
    is_block = i < n_blocks
    used = jnp.logical_and(is_block, nvalid_ref[jnp.minimum(i, n_blocks - 1)] > 0)
    gathers_next = i + 1 < n_blocks
    scatters_prev = i >= 1
    cur = i % 2

    def row_in(r, t):
        return pltpu.make_async_copy(xn_hbm.at[pl.ds(t, 1), :], xs_f32.at[pl.ds(r, 1), :], sem_in)

    def row_out(slot, r, t):
        return pltpu.make_async_copy(acc.at[slot, pl.ds(r, 1), :], y_hbm.at[pl.ds(t, 1), :], sem_out)

    def wait_all_in():
        pltpu.make_async_copy(xn_hbm.at[pl.ds(0, bm), :], xs_f32, sem_in).wait()

    def wait_all_out():
        pltpu.make_async_copy(acc.at[0], y_hbm.at[pl.ds(0, bm), :], sem_out).wait()

    def issue_gather():
        for k in range(rows_per_step):
            r = c * rows_per_step + k
            row_in(r, tok_ref[0, r]).start()

    def issue_scatter():
        for k in range(rows_per_step):
            r = c * rows_per_step + k
            row_out(1 - cur, r, dest_ref[0, r]).start()

    def compute():
        xs = xs_bf[...]
        g = jnp.dot(xs, wg_ref[...], preferred_element_type=F32)
        u = jnp.dot(xs, wu_ref[...], preferred_element_type=F32)
        hm = (g * jax.nn.sigmoid(g) * u).astype(BF16)
        part = jnp.dot(hm, wd_ref[...], preferred_element_type=F32)
        acc[cur] += part

    @pl.when(c == 0)
    def _():
        @pl.when(i == 0)
        def _():
            def start(r, carry):
                row_in(r, tok0_ref[0, r]).start()
                return carry
            lax.fori_loop(0, bm, start, 0)

        @pl.when(is_block)
        def _():
            wait_all_in()
            xs_bf[...] = xs_f32[...].astype(BF16)

        @pl.when(i >= 2)
        def _():
            wait_all_out()

        @pl.when(used)
        def _():
            acc[cur] = jnp.zeros((bm, D_MODEL), F32)

    fast = jnp.logical_and(jnp.logical_and(used, gathers_next), scatters_prev)

    @pl.when(fast)
    def _():
        issue_gather()
        issue_scatter()
        compute()

    @pl.when(jnp.logical_not(fast))
    def _():
        @pl.when(gathers_next)
        def _():
            issue_gather()

        @pl.when(scatters_prev)
        def _():
            issue_scatter()

        @pl.when(used)
        def _():
            compute()

    @pl.when(jnp.logical_and(used, c == n_chunk - 1))
    def _():
        acc[cur] = acc[cur] * jnp.tile(wb_ref[...], (1, D_MODEL // LANES))

    @pl.when(jnp.logical_and(i == n_blocks, c == n_chunk - 1))
    def _():
        wait_all_out()


def _moe(xn, wg, wu, wd, blk_expert, slot_tok, slot_dest, blk_valid, slot_wb, n_rows_out, bm, bc):
    n_slots = slot_tok.shape[0]
    n_blocks = n_slots // bm
    f = wg.shape[2]
    n_chunk = f // bc
    assert bm % n_chunk == 0 and n_blocks >= 2
    last = n_blocks - 1
    tok0_spec = pl.BlockSpec((None, 1, bm), lambda i, c, *_: (0, 0, 0), memory_space=pltpu.SMEM)
    tok_spec = pl.BlockSpec((None, 1, bm), lambda i, c, *_: (jnp.minimum(i + 1, last), 0, 0),
                            memory_space=pltpu.SMEM)
    dest_spec = pl.BlockSpec((None, 1, bm), lambda i, c, *_: (jnp.maximum(i - 1, 0), 0, 0),
                             memory_space=pltpu.SMEM)
    blk = lambda i: jnp.minimum(i, last)

    def wcol(i, c, nv):
        return jnp.where(jnp.logical_and(i <= last, nv[blk(i)] > 0), c, n_chunk - 1)

    grid_spec = pltpu.PrefetchScalarGridSpec(
        num_scalar_prefetch=2,
        grid=(n_blocks + 1, n_chunk),
        in_specs=[tok0_spec, tok_spec, dest_spec,
                  pl.BlockSpec(memory_space=pl.ANY),
                  pl.BlockSpec((None, D_MODEL, bc), lambda i, c, be, nv: (be[blk(i)], 0, wcol(i, c, nv))),
                  pl.BlockSpec((None, D_MODEL, bc), lambda i, c, be, nv: (be[blk(i)], 0, wcol(i, c, nv))),
                  pl.BlockSpec((None, bc, D_MODEL), lambda i, c, be, nv: (be[blk(i)], wcol(i, c, nv), 0)),
                  pl.BlockSpec((bm, LANES), lambda i, c, *_: (blk(i), 0))],
        out_specs=pl.BlockSpec(memory_space=pl.ANY),
        scratch_shapes=[pltpu.VMEM((bm, D_MODEL), F32),
                        pltpu.VMEM((bm, D_MODEL), BF16),
                        pltpu.VMEM((2, bm, D_MODEL), F32),
                        pltpu.SemaphoreType.DMA(()),
                        pltpu.SemaphoreType.DMA(())],
    )
    return pl.pallas_call(
        functools.partial(_moe_kernel, bm=bm, n_chunk=n_chunk, n_blocks=n_blocks),
        grid_spec=grid_spec,
        out_shape=jax.ShapeDtypeStruct((n_rows_out, D_MODEL), F32),
        compiler_params=_cparams(2),
        name="moe_experts",
    )(blk_expert, blk_valid, slot_tok.reshape(n_blocks, 1, bm), slot_tok.reshape(n_blocks, 1, bm),
      slot_dest.reshape(n_blocks, 1, bm), xn, wg, wu, wd, slot_wb)


def _final_kernel(x_ref, y0_ref, y1_ref, g_ref, o_ref):
    x = x_ref[...] + (y0_ref[...] + y1_ref[...])
    o_ref[...] = _rms(x, g_ref[...])


def _final(x, y, g, row0, rows, t, bm):
    rb, tb = row0 // bm, t // bm
    return pl.pallas_call(
        _final_kernel,
        grid=(rows // bm,),
        in_specs=[pl.BlockSpec((bm, D_MODEL), lambda i: (rb + i, 0)),
                  pl.BlockSpec((bm, D_MODEL), lambda i: (rb + i, 0)),
                  pl.BlockSpec((bm, D_MODEL), lambda i: (tb + rb + i, 0)),
                  pl.BlockSpec((1, D_MODEL), lambda i: (0, 0))],
        out_specs=pl.BlockSpec((bm, D_MODEL), lambda i: (i, 0)),
        out_shape=jax.ShapeDtypeStruct((rows, D_MODEL), F32),
        compiler_params=_cparams(1),
        name="final_norm",
    )(x, y, y, g)


def _rope_tables(n_pos):
    pos = jnp.arange(n_pos, dtype=F32)
    ones = jnp.ones((n_pos, 1), F32)
    zeros = jnp.zeros((n_pos, 1), F32)

    def angles(p, dims, theta):
        inv = jnp.power(jnp.float32(theta), -jnp.arange(0, dims, 2, dtype=F32) / dims)
        return p[:, None] * inv[None, :]

    ang = angles(pos, ROPE_DIMS, ROPE_THETA)
    rest = HEAD_DIM - ROPE_DIMS
    cos_a = jnp.concatenate([jnp.cos(ang), jnp.cos(ang), jnp.tile(ones, (1, rest))], axis=1)
    half = jnp.tile(zeros, (1, ROPE_DIMS // 2))
    tail = jnp.tile(zeros, (1, rest))
    lo_a = jnp.concatenate([-jnp.sin(ang), half, tail], axis=1)
    hi_a = jnp.concatenate([half, jnp.sin(ang), tail], axis=1)
    row = jnp.floor(pos / GRID_W)
    col = pos - row * GRID_W
    ar = angles(row, HEAD_DIM // 2, AXIAL_THETA)
    ac = angles(col, HEAD_DIM // 2, AXIAL_THETA)
    quarter = jnp.tile(zeros, (1, HEAD_DIM // 4))
    cos_b = jnp.concatenate([jnp.cos(ar), jnp.cos(ar), jnp.cos(ac), jnp.cos(ac)], axis=1)
    lo_b = jnp.concatenate([-jnp.sin(ar), quarter, -jnp.sin(ac), quarter], axis=1)
    hi_b = jnp.concatenate([quarter, jnp.sin(ar), quarter, jnp.sin(ac)], axis=1)
    return (cos_a, lo_a, hi_a), (cos_b, lo_b, hi_b)


def _routing_slots(route, t, bm):
    n_asg = t * TOP_K
    flat_e = route[:, :TOP_K].astype(jnp.int32).reshape(-1)
    flat_w = route[:, TOP_K:2 * TOP_K].reshape(-1)
    asg = jnp.arange(n_asg, dtype=jnp.int32)
    _, asg_sorted, w_sorted = lax.sort((flat_e, asg, lax.bitcast_convert_type(flat_w, jnp.int32)),
                                       num_keys=1, is_stable=True)
    experts = jnp.arange(N_EXPERTS, dtype=jnp.int32)
    counts = jnp.sum((flat_e[:, None] == experts[None, :]).astype(jnp.int32), axis=0)
    start = jnp.cumsum(counts) - counts
    padded = (counts + bm - 1) // bm * bm
    padded_end = jnp.cumsum(padded)
    padded_start = padded_end - padded
    n_slots = -(-(n_asg + N_EXPERTS * (bm - 1)) // bm) * bm
    n_blocks = n_slots // bm
    blk_start = jnp.arange(n_blocks, dtype=jnp.int32) * bm
    blk_expert = jnp.minimum(jnp.searchsorted(padded_end, blk_start, side='right'),
                             N_EXPERTS - 1).astype(jnp.int32)
    blk_valid = jnp.clip(padded_start[blk_expert] + counts[blk_expert] - blk_start, 0, bm)
    blk_valid = jnp.where(blk_start < padded_end[-1], blk_valid, 0).astype(jnp.int32)
    slot = jnp.arange(n_slots, dtype=jnp.int32)
    slot_e = jnp.repeat(blk_expert, bm)
    rank = slot - padded_start[slot_e]
    valid = jnp.logical_and(rank < counts[slot_e], slot < padded_end[-1])
    src = jnp.clip(start[slot_e] + rank, 0, n_asg - 1)
    picked = jnp.stack([asg_sorted, w_sorted], axis=1)[src]
    a = picked[:, 0]
    tok = a // TOP_K
    pads_before = slot - start[slot_e] - jnp.minimum(rank, counts[slot_e])
    slot_tok = jnp.where(valid, tok, 0)
    slot_dest = jnp.where(valid, (a % TOP_K) * t + tok, n_asg + pads_before)
    slot_w = jnp.where(valid, lax.bitcast_convert_type(picked[:, 1], F32), 0.0)
    slot_wb = jnp.broadcast_to(slot_w[:, None], (n_slots, LANES))
    return blk_expert, slot_tok, slot_dest, blk_valid, slot_wb


def _tile(n, pref):
    while n % pref:
        pref //= 2
    return pref


def _trunk(x_prompt, x_sample, attn_norm, w_in, b_gate, q_norm, k_norm, w_proj_a, w_proj_b,
           w_out, ffn_norm, ffn_w_gate, ffn_w_up, ffn_w_down, moe_router, moe_w_gate,
           moe_w_up, moe_w_down, final_norm):
    bp, sp, _ = x_prompt.shape
    bs, ss, _ = x_sample.shape
    tp, ts = bp * sp, bs * ss
    t = tp + ts
    depth = w_in.shape[0]
    assert depth == 2, "dense-FFN layer followed by a final MoE layer"
    assert ss % sp == 0 and tp % ss == 0, "blocks of ss tokens must hold whole sequences"
    sets = ((0, tp), (tp, ts))

    seq_unit = math.gcd(sp, ss)
    bm = _tile(seq_unit, 1024)
    bm_small = _tile(seq_unit, 512)
    n_p_blk, n_pb, n_sb = tp // bm, sp // bm, ss // bm

    def pos_blk(i, j):
        return (jnp.where(i < n_p_blk, i % n_pb, (i - n_p_blk) % n_sb), 0)

    tab_a, tab_b = _rope_tables(max(sp, ss))

    row2 = lambda v: v.reshape(1, -1).astype(F32)
    x, xn = _norm_in(x_prompt.reshape(tp, D_MODEL), x_sample.reshape(ts, D_MODEL),
                     row2(attn_norm[0]), bm_small)

    out = None
    for layer in range(depth):
        qk_a, v_a, q_b, k_b, v_b, gates = _proj_in(
            xn, w_in[layer].astype(BF16), tab_a, tab_b, row2(q_norm[layer]), row2(k_norm[layer]),
            row2(b_gate[layer]), pos_blk, bm)

        oa, lse = [], []
        for g in range(A_N_GROUPS):
            o_g, lse_g = _attn_a(qk_a, v_a, g, tp, sp, ss)
            oa.append(o_g)
            lse.append(lse_g)
        ob = _attn_b(q_b, k_b, v_b, tp, sp, ss)

        merged = _merge(oa, lse, ob, gates, w_proj_a[layer].astype(BF16),
                        w_proj_b[layer].astype(BF16), bm_small // 2)
        moe_layer = layer % 2 == 1
        row_spec = pl.BlockSpec((bm_small, D_MODEL), lambda i, j: (i, 0))
        vec_spec = pl.BlockSpec((1, D_MODEL), lambda i, j: (0, 0))
        if moe_layer:
            wr = moe_router[layer // 2].astype(F32).T
            x, xn_moe, route = _matmul(
                merged, w_out[layer].astype(BF16), bm=bm_small, bn=D_MODEL,
                epilogue=_epi_residual_route,
                extras=[x, row2(ffn_norm[layer]), wr],
                extra_specs=[row_spec, vec_spec, pl.BlockSpec((N_EXPERTS, D_MODEL), lambda i, j: (0, 0))],
                out_shape=[jax.ShapeDtypeStruct((t, D_MODEL), F32),
                           jax.ShapeDtypeStruct((t, D_MODEL), F32),
                           jax.ShapeDtypeStruct((t, LANES), F32)],
                out_specs=[row_spec, row_spec, pl.BlockSpec((bm_small, LANES), lambda i, j: (i, 0))],
                name="proj_out_route")
        else:
            x, xn_ffn = _matmul(
                merged, w_out[layer].astype(BF16), bm=bm_small, bn=D_MODEL,
                epilogue=_epi_residual_norm,
                extras=[x, row2(ffn_norm[layer])],
                extra_specs=[row_spec, vec_spec],
                out_shape=[jax.ShapeDtypeStruct((t, D_MODEL), F32),
                           jax.ShapeDtypeStruct((t, D_MODEL), BF16)],
                out_specs=[row_spec, row_spec],
                name="proj_out")

        j = layer // 2
        if layer % 2 == 0:
            x, xn = _ffn(xn_ffn, ffn_w_gate[j].astype(BF16), ffn_w_up[j].astype(BF16),
                         ffn_w_down[j].astype(BF16), x, row2(attn_norm[layer + 1]), bm_small, 512)
        else:
            meta = _routing_slots(route, t, MOE_BM)
            n_rows_out = meta[1].shape[0]
            y = _moe(xn_moe, moe_w_gate[j].astype(BF16), moe_w_up[j].astype(BF16),
                     moe_w_down[j].astype(BF16), *meta, n_rows_out, MOE_BM, MOE_BC)
            out = tuple(_final(x, y, row2(final_norm), row0, rows, t, bm_small)
                        for row0, rows in sets)
    y_prompt, y_sample = out
    return y_prompt.reshape(bp, sp, D_MODEL), y_sample.reshape(bs, ss, D_MODEL)


def kernel(x_prompt, x_sample, attn_norm, w_in, b_gate, q_norm, k_norm, w_proj_a, w_proj_b,
           w_out, ffn_norm, ffn_w_gate, ffn_w_up, ffn_w_down, moe_router, moe_w_gate,
           moe_w_up, moe_w_down, final_norm):
    return _trunk(x_prompt, x_sample, attn_norm, w_in, b_gate, q_norm, k_norm, w_proj_a,
                  w_proj_b, w_out, ffn_norm, ffn_w_gate, ffn_w_up, ffn_w_down, moe_router,
                  moe_w_gate, moe_w_up, moe_w_down, final_norm)
```

```python
import functools
import math

import jax
import jax.numpy as jnp
from jax import lax
from jax.experimental import pallas as pl
from jax.experimental.pallas import tpu as pltpu

F32 = jnp.float32
BF16 = jnp.bfloat16

D_MODEL = 2048
HEAD_DIM = 128
ATTN_SCALE = HEAD_DIM ** -0.5
EPS = 1e-6
NEG = -1e30

A_WINDOWS = (128, 512, 2048)
A_DILATIONS = (1, 4, 16)
A_N_GROUPS = 3
A_HEADS = 4
A_WIDTH = A_HEADS * HEAD_DIM
A_QK = 2 * A_N_GROUPS * A_WIDTH
A_QKV = 3 * A_N_GROUPS * A_WIDTH
A_RADIUS = 64
ROPE_THETA = 500000.0
ROPE_DIMS = HEAD_DIM // 4

B_Q_HEADS = 8
B_KV_HEADS = 2
B_GROUP = B_Q_HEADS // B_KV_HEADS
B_WIDTH = B_Q_HEADS * HEAD_DIM
B_KVW = B_KV_HEADS * HEAD_DIM
B_QKV = B_WIDTH + 2 * B_KVW
AXIAL_THETA = 10000.0
GRID_W = 64

C_GATE = 2 * D_MODEL
N_EXPERTS = 8
TOP_K = 2

V7X_VMEM_BYTES = 64 * 1024 * 1024
VMEM_LIMIT = V7X_VMEM_BYTES - 8 * 1024 * 1024
LANES = 128

LOG2E = math.log2(math.e)


def _cparams(n_axes, vmem=VMEM_LIMIT):
    return pltpu.CompilerParams(dimension_semantics=("arbitrary",) * n_axes,
                                vmem_limit_bytes=vmem)


def _rms(x, g):
    return x * lax.rsqrt(jnp.mean(x * x, axis=-1, keepdims=True) + EPS) * g


def _rope128(x, c, s_lo, s_hi, w):
    return x * c + pltpu.roll(x, LANES - w, 1) * s_lo + pltpu.roll(x, w, 1) * s_hi


def _norm_in_kernel(xp_ref, xs_ref, g_ref, x_ref, xn_ref, *, n_p):
    i = pl.program_id(0)

    def emit(src):
        x = src[...]
        x_ref[...] = x
        xn_ref[...] = _rms(x, g_ref[...]).astype(BF16)

    @pl.when(i < n_p)
    def _():
        emit(xp_ref)

    @pl.when(i >= n_p)
    def _():
        emit(xs_ref)


def _norm_in(xp, xs, g, bm):
    tp, ts = xp.shape[0], xs.shape[0]
    n_p, n_s = tp // bm, ts // bm
    t = tp + ts
    return pl.pallas_call(
        functools.partial(_norm_in_kernel, n_p=n_p),
        grid=(n_p + n_s,),
        in_specs=[pl.BlockSpec((bm, D_MODEL), lambda i: (jnp.minimum(i, n_p - 1), 0)),
                  pl.BlockSpec((bm, D_MODEL), lambda i: (jnp.maximum(i - n_p, 0), 0)),
                  pl.BlockSpec((1, D_MODEL), lambda i: (0, 0))],
        out_specs=[pl.BlockSpec((bm, D_MODEL), lambda i: (i, 0)),
                   pl.BlockSpec((bm, D_MODEL), lambda i: (i, 0))],
        out_shape=[jax.ShapeDtypeStruct((t, D_MODEL), F32),
                   jax.ShapeDtypeStruct((t, D_MODEL), BF16)],
        compiler_params=_cparams(1),
        name="norm_in",
    )(xp, xs, g)


def _mm_kernel(a_ref, w_ref, *refs, epilogue, n_extra):
    acc = jnp.dot(a_ref[...], w_ref[...], preferred_element_type=F32)
    epilogue(acc, refs[:n_extra], refs[n_extra:])


def _matmul(a, w, *, bm, bn, epilogue, extras=(), extra_specs=(), out_shape, out_specs, name):
    m, k = a.shape
    n = w.shape[1]
    assert m % bm == 0 and n % bn == 0, (m, n, bm, bn)
    return pl.pallas_call(
        functools.partial(_mm_kernel, epilogue=epilogue, n_extra=len(extras)),
        grid=(m // bm, n // bn),
        in_specs=[pl.BlockSpec((bm, k), lambda i, j: (i, 0)),
                  pl.BlockSpec((k, bn), lambda i, j: (0, j)),
                  *extra_specs],
        out_specs=out_specs,
        out_shape=out_shape,
        compiler_params=_cparams(2),
        name=name,
    )(a, w, *extras)


def _epi_residual_norm(acc, extras, outs):
    x = extras[0][...] + acc
    outs[0][...] = x
    outs[1][...] = _rms(x, extras[1][...]).astype(outs[1].dtype)


PROJ_BN = 2 * B_KVW
_J_VA = A_QK // PROJ_BN
_J_QB = A_QKV // PROJ_BN
_J_KV = (A_QKV + B_WIDTH) // PROJ_BN
_J_GATE = (A_QKV + B_QKV) // PROJ_BN
_J_END = (A_QKV + B_QKV + C_GATE) // PROJ_BN


def _proj_in_kernel(xn_ref, w_ref, ca_ref, la_ref, ha_ref, cb_ref, lb_ref, hb_ref,
                    gq_ref, gk_ref, bias_ref,
                    qka_ref, va_ref, qb_ref, kb_ref, vb_ref, gate_ref, acc_ref):
    j = pl.program_id(1)
    p = j - 1

    def matmul():
        acc_ref[...] = jnp.dot(xn_ref[...], w_ref[...], preferred_element_type=F32)

    def heads(n):
        return [slice(h * LANES, (h + 1) * LANES) for h in range(n // LANES)]

    def norm_rope_b(x, g_ref):
        return _rope128(_rms(x, g_ref[...]), cb_ref[...], lb_ref[...], hb_ref[...], HEAD_DIM // 4)

    def epi_qk_a():
        for sl in heads(PROJ_BN):
            qka_ref[:, sl] = _rope128(acc_ref[:, sl], ca_ref[...], la_ref[...], ha_ref[...],
                                      ROPE_DIMS // 2)

    def epi_v_a():
        va_ref[...] = acc_ref[...]

    def epi_q_b():
        tq = qb_ref.shape[2]
        for h, sl in enumerate(heads(PROJ_BN)):
            yt = (norm_rope_b(acc_ref[:, sl], gq_ref) * (ATTN_SCALE * LOG2E)).T
            for n in range(qb_ref.shape[0]):
                qb_ref[n, h * HEAD_DIM:(h + 1) * HEAD_DIM, :] = (
                    yt[:, n * tq:(n + 1) * tq].astype(BF16))

    def epi_kv_b():
        for sl in heads(B_KVW):
            kb_ref[:, sl] = norm_rope_b(acc_ref[:, sl], gk_ref).astype(BF16)
        tk = vb_ref.shape[2]
        vt = acc_ref[:, B_KVW:].T
        for n in range(vb_ref.shape[0]):
            vb_ref[n] = vt[:, n * tk:(n + 1) * tk].astype(BF16)

    def epi_gate():
        gate_ref[...] = (acc_ref[...] + bias_ref[...]).astype(BF16)

    def between(lo, hi):
        return jnp.logical_and(p >= lo, p < hi)

    @pl.when(j == 0)
    def _():
        matmul()

    for lo, hi, epilogue in ((0, _J_VA, epi_qk_a), (_J_VA, _J_QB, epi_v_a), (_J_QB, _J_KV, epi_q_b),
                             (_J_KV, _J_GATE, epi_kv_b), (_J_GATE, _J_END - 1, epi_gate)):
        @pl.when(between(lo, hi))
        def _(epilogue=epilogue):
            epilogue()
            matmul()

    @pl.when(p == _J_END - 1)
    def _():
        epi_gate()


def _proj_in(xn, w, tab_a, tab_b, gq, gk, bias, pos_blk, bm):
    t = xn.shape[0]
    bn = PROJ_BN
    tq, tk = ATTN_B_TQ, ATTN_B_TK
    assert bm % tq == 0 and bm % tk == 0

    def cols(lo, hi):
        return lambda i, j: (i, jnp.clip(j - 1 - lo, 0, hi - lo - 1))

    tab_spec = pl.BlockSpec((bm, LANES), pos_blk)
    vec_spec = pl.BlockSpec((1, LANES), lambda i, j: (0, 0))
    return pl.pallas_call(
        _proj_in_kernel,
        grid=(t // bm, _J_END + 1),
        in_specs=[pl.BlockSpec((bm, D_MODEL), lambda i, j: (i, 0)),
                  pl.BlockSpec((D_MODEL, bn), lambda i, j: (0, jnp.minimum(j, _J_END - 1))),
                  *([tab_spec] * 6), vec_spec, vec_spec,
                  pl.BlockSpec((1, bn), lambda i, j: (0, jnp.clip(j - 1 - _J_GATE, 0, _J_END - _J_GATE - 1)))],
        out_specs=[pl.BlockSpec((bm, bn), cols(0, _J_VA)),
                   pl.BlockSpec((bm, bn), cols(_J_VA, _J_QB)),
                   pl.BlockSpec((bm // tq, bn, tq),
                                lambda i, j: (i, jnp.clip(j - 1 - _J_QB, 0, _J_KV - _J_QB - 1), 0)),
                   pl.BlockSpec((bm, B_KVW), lambda i, j: (i, 0)),
                   pl.BlockSpec((bm // tk, B_KVW, tk), lambda i, j: (i, 0, 0)),
                   pl.BlockSpec((bm, bn), cols(_J_GATE, _J_END))],
        out_shape=[jax.ShapeDtypeStruct((t, A_QK), F32),
                   jax.ShapeDtypeStruct((t, A_QKV - A_QK), F32),
                   jax.ShapeDtypeStruct((t // tq, B_WIDTH, tq), BF16),
                   jax.ShapeDtypeStruct((t, B_KVW), BF16),
                   jax.ShapeDtypeStruct((t // tk, B_KVW, tk), BF16),
                   jax.ShapeDtypeStruct((t, C_GATE), BF16)],
        scratch_shapes=[pltpu.VMEM((bm, bn), F32)],
        compiler_params=_cparams(2),
        name="proj_in",
    )(xn, w, *tab_a, *tab_b, gq, gk, bias)


ATTN_A_UNROLL = 8
ATTN_A_TQ = 256
ATTN_A_WHOLE = 512


def _attn_a_tiles(l_sub):
    if l_sub <= ATTN_A_WHOLE:
        return l_sub, l_sub
    assert l_sub % ATTN_A_TQ == 0
    return ATTN_A_TQ, ATTN_A_TQ + 2 * A_RADIUS


def _band_bias(tq, win):
    r = jnp.arange(tq, dtype=jnp.int32)[:, None]
    c = jnp.arange(win, dtype=jnp.int32)[None, :]
    offs = (0, A_RADIUS, win - tq)
    return jnp.stack([jnp.where(jnp.abs(c - r - off) <= A_RADIUS, 0.0, NEG).astype(F32)
                      for off in offs])


def _attn_a_kernel(q_ref, k_ref, v_ref, bias_p_ref, bias_s_ref, o_ref, lse_ref, *,
                   d, n_short_blk, sp, ss):
    def rows(start, size):
        if d == 1:
            return pl.ds(start, size)
        return pl.ds(start, size, stride=d)

    def run(seq, n_seq, bias_ref):
        l_sub = seq // d
        tq, win = _attn_a_tiles(l_sub)
        n_tile = l_sub // tq

        def tile(idx, carry):
            base = (idx // (d * n_tile)) * seq
            r = (idx // n_tile) % d
            ti = idx % n_tile
            t0 = ti * tq
            ks = jnp.clip(t0 - A_RADIUS, 0, l_sub - win)
            if n_tile == 1:
                bias = bias_ref[0]
            else:
                bias = bias_ref[jnp.where(ti == 0, 0, jnp.where(ti == n_tile - 1, 2, 1))]
            q = (q_ref[rows(base + r + t0 * d, tq), :] * ATTN_SCALE).astype(BF16)
            k = k_ref[rows(base + r + ks * d, win), :].astype(BF16)
            v = v_ref[rows(base + r + ks * d, win), :].astype(BF16)
            s = lax.dot_general(q, k, (((1,), (1,)), ((), ())), preferred_element_type=F32) + bias
            m = jnp.max(s, axis=-1, keepdims=True)
            p = jnp.exp(s - m)
            den = jnp.sum(p, axis=-1, keepdims=True)
            o = jnp.dot(p.astype(BF16), v, preferred_element_type=F32) / den
            lse = m + jnp.log(den)
            o_ref[rows(base + r + t0 * d, tq), :] = o
            lse_ref[rows(base + r + t0 * d, tq), :] = jnp.broadcast_to(lse, (tq, LANES))
            return carry

        lax.fori_loop(0, n_seq * d * n_tile, tile, 0, unroll=ATTN_A_UNROLL)

    if sp == ss:
        run(ss, 1, bias_s_ref)
    else:
        blk = pl.program_id(0)

        @pl.when(blk < n_short_blk)
        def _():
            run(sp, ss // sp, bias_p_ref)

        @pl.when(blk >= n_short_blk)
        def _():
            run(ss, 1, bias_s_ref)


def _attn_a(qk, v, g, tp, sp, ss):
    t = qk.shape[0]
    d = A_DILATIONS[g]
    biases = [_band_bias(*_attn_a_tiles(seq // d)) for seq in (sp, ss)]
    kern = functools.partial(_attn_a_kernel, d=d, n_short_blk=tp // ss, sp=sp, ss=ss)
    return pl.pallas_call(
        kern,
        grid=(t // ss, A_HEADS),
        in_specs=[pl.BlockSpec((ss, LANES), lambda b, h: (b, g * A_HEADS + h)),
                  pl.BlockSpec((ss, LANES), lambda b, h: (b, (A_N_GROUPS + g) * A_HEADS + h)),
                  pl.BlockSpec((ss, LANES), lambda b, h: (b, g * A_HEADS + h)),
                  *[pl.BlockSpec(bias.shape, lambda b, h: (0, 0, 0)) for bias in biases]],
        out_specs=[pl.BlockSpec((ss, LANES), lambda b, h: (b, h))] * 2,
        out_shape=[jax.ShapeDtypeStruct((t, A_WIDTH), F32)] * 2,
        compiler_params=_cparams(2),
        name=f"attn_a{g}",
    )(qk, qk, v, *biases)


ATTN_B_TQ = 512
ATTN_B_TK = 1024


def _attn_b_kernel(qt_ref, qt_next_ref, k_ref, vt_ref, o_ref, s_ref, *, tq, tk, tp, sp, ss):
    i = pl.program_id(1)
    tiles_per_block = ss // tq

    def first_chunk(row):
        return jnp.where(row < tp, ((row % ss) // sp) * sp, 0) // tk

    row = i * tq
    first = first_chunk(row)
    n_chunk = jnp.where(row < tp, sp // tk, ss // tk)
    last = first + n_chunk - 1

    def heads_on_lanes(ref):
        return jnp.concatenate([ref[h * HEAD_DIM:(h + 1) * HEAD_DIM, :] for h in range(B_GROUP)],
                               axis=1)

    qt = heads_on_lanes(qt_ref)
    cols = B_GROUP * tq

    def scores(c, slot, q):
        k = k_ref[pl.ds(pl.multiple_of(c * tk, tk), tk), :]
        s_ref[slot] = jnp.dot(k, q, preferred_element_type=F32)

    def chunk(c, slot, carry, prefetch):
        prefetch()
        m, l, acc = carry
        st = s_ref[slot]
        m_new = jnp.maximum(m, jnp.max(st, axis=0, keepdims=True))
        alpha = jnp.exp2(m - m_new)
        pt = jnp.exp2(st - m_new)
        l = alpha * l + jnp.sum(pt, axis=0, keepdims=True)
        acc = alpha * acc + jnp.dot(vt_ref[c], pt.astype(BF16), preferred_element_type=F32)
        return m_new, l, acc

    def body(it, carry):
        for slot in range(2):
            c = first + 2 * it + slot
            carry = chunk(c, slot, carry, lambda c=c, slot=slot: scores(c + 1, 1 - slot, qt))
        return carry

    @pl.when(i % tiles_per_block == 0)
    def _():
        scores(first, 0, qt)

    init = (jnp.full((1, cols), NEG, F32), jnp.zeros((1, cols), F32),
            jnp.zeros((HEAD_DIM, cols), F32))
    carry = lax.fori_loop(0, n_chunk // 2 - 1, body, init)
    carry = chunk(last - 1, 0, carry, lambda: scores(last, 1, qt))
    next_first = first_chunk(row + tq)
    _, l, acc = chunk(last, 1, carry,
                      lambda: scores(next_first, 0, heads_on_lanes(qt_next_ref)))
    ot = acc / l
    for h in range(B_GROUP):
        o_ref[:, h * LANES:(h + 1) * LANES] = ot[:, h * tq:(h + 1) * tq].T.astype(o_ref.dtype)


def _attn_b(qt, k, vt, tp, sp, ss):
    tq, tk = ATTN_B_TQ, ATTN_B_TK
    t = k.shape[0]
    n_tile = t // tq
    gw = B_GROUP * HEAD_DIM
    assert sp % (2 * tk) == 0 and ss % (2 * tk) == 0 and sp % tq == 0
    return pl.pallas_call(
        functools.partial(_attn_b_kernel, tq=tq, tk=tk, tp=tp, sp=sp, ss=ss),
        grid=(B_KV_HEADS, n_tile),
        scratch_shapes=[pltpu.VMEM((2, tk, gw // HEAD_DIM * tq), F32)],
        in_specs=[pl.BlockSpec((None, gw, tq), lambda kv, i: (i, kv, 0)),
                  pl.BlockSpec((None, gw, tq), lambda kv, i: (jnp.minimum(i + 1, n_tile - 1), kv, 0)),
                  pl.BlockSpec((ss, HEAD_DIM), lambda kv, i: (i // (ss // tq), kv)),
                  pl.BlockSpec((ss // tk, HEAD_DIM, tk), lambda kv, i: (i // (ss // tq), kv, 0))],
        out_specs=pl.BlockSpec((tq, gw), lambda kv, i: (i, kv)),
        out_shape=jax.ShapeDtypeStruct((t, B_WIDTH), BF16),
        compiler_params=_cparams(2),
        name="attn_b",
    )(qt, qt, k, vt)


def _merge_kernel(o0, o1, o2, l0, l1, l2, ob_ref, gate_ref, wpa_ref, wpb_ref, out_ref):
    lse = [l0[...], l1[...], l2[...]]
    top = jnp.maximum(jnp.maximum(lse[0], lse[1]), lse[2])
    e = [jnp.exp(x - top) for x in lse]
    oa = (e[0] * o0[...] + e[1] * o1[...] + e[2] * o2[...]) / (e[0] + e[1] + e[2])
    pa = jnp.dot(oa.astype(BF16), wpa_ref[...], preferred_element_type=F32)
    pb = jnp.dot(ob_ref[...], wpb_ref[...], preferred_element_type=F32)
    ga = jax.nn.sigmoid(gate_ref[:, :D_MODEL].astype(F32))
    gb = jax.nn.sigmoid(gate_ref[:, D_MODEL:].astype(F32))
    out_ref[...] = (ga * pa + gb * pb).astype(out_ref.dtype)


def _merge(oa, lse, ob, gates, wpa, wpb, bm):
    t = ob.shape[0]
    row = lambda w: pl.BlockSpec((bm, w), lambda i: (i, 0))
    const = lambda s: pl.BlockSpec(s, lambda i: (0, 0))
    return pl.pallas_call(
        _merge_kernel,
        grid=(t // bm,),
        in_specs=[row(A_WIDTH)] * 6 + [row(B_WIDTH), row(C_GATE),
                                       const((A_WIDTH, D_MODEL)), const((B_WIDTH, D_MODEL))],
        out_specs=row(D_MODEL),
        out_shape=jax.ShapeDtypeStruct((t, D_MODEL), BF16),
        compiler_params=_cparams(1),
        name="merge",
    )(*oa, *lse, ob, gates, wpa, wpb)


def _ffn_kernel(xn_ref, wg_ref, wu_ref, wd_ref, x_ref, gn_ref, o_ref, on_ref, *, n_chunk):
    c = pl.program_id(1)

    @pl.when(c == 0)
    def _():
        o_ref[...] = x_ref[...]

    xn = xn_ref[...]
    g = jnp.dot(xn, wg_ref[...], preferred_element_type=F32)
    u = jnp.dot(xn, wu_ref[...], preferred_element_type=F32)
    hm = (g * jax.nn.sigmoid(g) * u).astype(BF16)
    o_ref[...] += jnp.dot(hm, wd_ref[...], preferred_element_type=F32)

    @pl.when(c == n_chunk - 1)
    def _():
        on_ref[...] = _rms(o_ref[...], gn_ref[...]).astype(on_ref.dtype)


def _ffn(xn, wg, wu, wd, x, g_next, bm, bc):
    t = xn.shape[0]
    f = wg.shape[1]
    n_chunk = f // bc
    row = lambda: pl.BlockSpec((bm, D_MODEL), lambda i, c: (i, 0))
    return pl.pallas_call(
        functools.partial(_ffn_kernel, n_chunk=n_chunk),
        grid=(t // bm, n_chunk),
        in_specs=[row(),
                  pl.BlockSpec((D_MODEL, bc), lambda i, c: (0, c)),
                  pl.BlockSpec((D_MODEL, bc), lambda i, c: (0, c)),
                  pl.BlockSpec((bc, D_MODEL), lambda i, c: (c, 0)),
                  row(),
                  pl.BlockSpec((1, D_MODEL), lambda i, c: (0, 0))],
        out_specs=[row(), row()],
        out_shape=[jax.ShapeDtypeStruct((t, D_MODEL), F32),
                   jax.ShapeDtypeStruct((t, D_MODEL), BF16)],
        compiler_params=_cparams(2),
        name="ffn_dense",
    )(xn, wg, wu, wd, x, g_next)


def _route(xn, wrt_ref):
    logits = [jnp.sum(xn * wrt_ref[e:e + 1, :], axis=-1, keepdims=True) for e in range(N_EXPERTS)]
    v1, i1 = logits[0], jnp.zeros(logits[0].shape, jnp.int32)
    for e in range(1, N_EXPERTS):
        better = logits[e] > v1
        i1 = jnp.where(better, e, i1)
        v1 = jnp.where(better, logits[e], v1)
    v2 = jnp.full(v1.shape, -jnp.inf, F32)
    i2 = jnp.zeros(i1.shape, jnp.int32)
    for e in range(N_EXPERTS):
        cand = jnp.where(i1 == e, -jnp.inf, logits[e])
        better = cand > v2
        i2 = jnp.where(better, e, i2)
        v2 = jnp.where(better, cand, v2)
    e2 = jnp.exp(v2 - v1)
    w1 = 1.0 / (1.0 + e2)
    w2 = e2 / (1.0 + e2)
    lane = lax.broadcasted_iota(jnp.int32, (xn.shape[0], LANES), 1)
    return jnp.where(lane == 0, i1.astype(F32),
                     jnp.where(lane == 1, i2.astype(F32),
                               jnp.where(lane == 2, w1, jnp.where(lane == 3, w2, 0.0))))


def _epi_residual_route(acc, extras, outs):
    x = extras[0][...] + acc
    xn = _rms(x, extras[1][...])
    outs[0][...] = x
    outs[1][...] = xn
    outs[2][...] = _route(xn, extras[2])


MOE_BM = 672
MOE_BC = 512


def _moe_kernel(be_ref, nvalid_ref,
                tok0_ref, tok_ref, dest_ref, xn_hbm, wg_ref, wu_ref, wd_ref, wb_ref, y_hbm,
                xs_f32, xs_bf, acc, sem_in, sem_out, *, bm, n_chunk, n_blocks):
    i = pl.program_id(0)
    c = pl.program_id(1)
    rows_per_step = bm // n_chunk
    is_block = i < n_blocks
    used = jnp.logical_and(is_block, nvalid_ref[jnp.minimum(i, n_blocks - 1)] > 0)
    gathers_next = i + 1 < n_blocks
    scatters_prev = i >= 1
    cur = i % 2

    def row_in(r, t):
        return pltpu.make_async_copy(xn_hbm.at[pl.ds(t, 1), :], xs_f32.at[pl.ds(r, 1), :], sem_in)

    def row_out(slot, r, t):
        return pltpu.make_async_copy(acc.at[slot, pl.ds(r, 1), :], y_hbm.at[pl.ds(t, 1), :], sem_out)

    def wait_all_in():
        pltpu.make_async_copy(xn_hbm.at[pl.ds(0, bm), :], xs_f32, sem_in).wait()

    def wait_all_out():
        pltpu.make_async_copy(acc.at[0], y_hbm.at[pl.ds(0, bm), :], sem_out).wait()

    def issue_gather():
        for k in range(rows_per_step):
            r = c * rows_per_step + k
            row_in(r, tok_ref[0, r]).start()

    def issue_scatter():
        for k in range(rows_per_step):
            r = c * rows_per_step + k
            row_out(1 - cur, r, dest_ref[0, r]).start()

    def compute():
        xs = xs_bf[...]
        g = jnp.dot(xs, wg_ref[...].astype(BF16), preferred_element_type=F32)
        u = jnp.dot(xs, wu_ref[...].astype(BF16), preferred_element_type=F32)
        hm = (g * jax.nn.sigmoid(g) * u).astype(BF16)
        part = jnp.dot(hm, wd_ref[...].astype(BF16), preferred_element_type=F32)
        acc[cur] += part

    @pl.when(c == 0)
    def _():
        @pl.when(i == 0)
        def _():
            def start(r, carry):
                row_in(r, tok0_ref[0, r]).start()
                return carry
            lax.fori_loop(0, bm, start, 0)

        @pl.when(is_block)
        def _():
            wait_all_in()
            xs_bf[...] = xs_f32[...].astype(BF16)

        @pl.when(i >= 2)
        def _():
            wait_all_out()

        @pl.when(used)
        def _():
            acc[cur] = jnp.zeros((bm, D_MODEL), F32)

    fast = jnp.logical_and(jnp.logical_and(used, gathers_next), scatters_prev)

    @pl.when(fast)
    def _():
        issue_gather()
        issue_scatter()
        compute()

    @pl.when(jnp.logical_not(fast))
    def _():
        @pl.when(gathers_next)
        def _():
            issue_gather()

        @pl.when(scatters_prev)
        def _():
            issue_scatter()

        @pl.when(used)
        def _():
            compute()

    @pl.when(jnp.logical_and(used, c == n_chunk - 1))
    def _():
        acc[cur] = acc[cur] * jnp.tile(wb_ref[...], (1, D_MODEL // LANES))

    @pl.when(jnp.logical_and(i == n_blocks, c == n_chunk - 1))
    def _():
        wait_all_out()


def _moe(xn, wg, wu, wd, blk_expert, slot_tok, slot_dest, blk_valid, slot_wb, n_rows_out, bm, bc):
    n_slots = slot_tok.shape[0]
    n_blocks = n_slots // bm
    f = wg.shape[2]
    n_chunk = f // bc
    assert bm % n_chunk == 0 and n_blocks >= 2
    last = n_blocks - 1
    tok0_spec = pl.BlockSpec((None, 1, bm), lambda i, c, *_: (0, 0, 0), memory_space=pltpu.SMEM)
    tok_spec = pl.BlockSpec((None, 1, bm), lambda i, c, *_: (jnp.minimum(i + 1, last), 0, 0),
                            memory_space=pltpu.SMEM)
    dest_spec = pl.BlockSpec((None, 1, bm), lambda i, c, *_: (jnp.maximum(i - 1, 0), 0, 0),
                             memory_space=pltpu.SMEM)
    blk = lambda i: jnp.minimum(i, last)

    def wcol(i, c, nv):
        return jnp.where(jnp.logical_and(i <= last, nv[blk(i)] > 0), c, n_chunk - 1)

    grid_spec = pltpu.PrefetchScalarGridSpec(
        num_scalar_prefetch=2,
        grid=(n_blocks + 1, n_chunk),
        in_specs=[tok0_spec, tok_spec, dest_spec,
                  pl.BlockSpec(memory_space=pl.ANY),
                  pl.BlockSpec((None, D_MODEL, bc), lambda i, c, be, nv: (be[blk(i)], 0, wcol(i, c, nv))),
                  pl.BlockSpec((None, D_MODEL, bc), lambda i, c, be, nv: (be[blk(i)], 0, wcol(i, c, nv))),
                  pl.BlockSpec((None, bc, D_MODEL), lambda i, c, be, nv: (be[blk(i)], wcol(i, c, nv), 0)),
                  pl.BlockSpec((bm, LANES), lambda i, c, *_: (blk(i), 0))],
        out_specs=pl.BlockSpec(memory_space=pl.ANY),
        scratch_shapes=[pltpu.VMEM((bm, D_MODEL), F32),
                        pltpu.VMEM((bm, D_MODEL), BF16),
                        pltpu.VMEM((2, bm, D_MODEL), F32),
                        pltpu.SemaphoreType.DMA(()),
                        pltpu.SemaphoreType.DMA(())],
    )
    return pl.pallas_call(
        functools.partial(_moe_kernel, bm=bm, n_chunk=n_chunk, n_blocks=n_blocks),
        grid_spec=grid_spec,
        out_shape=jax.ShapeDtypeStruct((n_rows_out, D_MODEL), F32),
        compiler_params=_cparams(2),
        name="moe_experts",
    )(blk_expert, blk_valid, slot_tok.reshape(n_blocks, 1, bm), slot_tok.reshape(n_blocks, 1, bm),
      slot_dest.reshape(n_blocks, 1, bm), xn, wg, wu, wd, slot_wb)


def _final_kernel(x_ref, y0_ref, y1_ref, g_ref, o_ref):
    x = x_ref[...] + (y0_ref[...] + y1_ref[...])
    o_ref[...] = _rms(x, g_ref[...])


def _final(x, y, g, row0, rows, t, bm):
    rb, tb = row0 // bm, t // bm
    return pl.pallas_call(
        _final_kernel,
        grid=(rows // bm,),
        in_specs=[pl.BlockSpec((bm, D_MODEL), lambda i: (rb + i, 0)),
                  pl.BlockSpec((bm, D_MODEL), lambda i: (rb + i, 0)),
                  pl.BlockSpec((bm, D_MODEL), lambda i: (tb + rb + i, 0)),
                  pl.BlockSpec((1, D_MODEL), lambda i: (0, 0))],
        out_specs=pl.BlockSpec((bm, D_MODEL), lambda i: (i, 0)),
        out_shape=jax.ShapeDtypeStruct((rows, D_MODEL), F32),
        compiler_params=_cparams(1),
        name="final_norm",
    )(x, y, y, g)


def _rope_tables(n_pos):
    pos = jnp.arange(n_pos, dtype=F32)
    ones = jnp.ones((n_pos, 1), F32)
    zeros = jnp.zeros((n_pos, 1), F32)

    def angles(p, dims, theta):
        inv = jnp.power(jnp.float32(theta), -jnp.arange(0, dims, 2, dtype=F32) / dims)
        return p[:, None] * inv[None, :]

    ang = angles(pos, ROPE_DIMS, ROPE_THETA)
    rest = HEAD_DIM - ROPE_DIMS
    cos_a = jnp.concatenate([jnp.cos(ang), jnp.cos(ang), jnp.tile(ones, (1, rest))], axis=1)
    half = jnp.tile(zeros, (1, ROPE_DIMS // 2))
    tail = jnp.tile(zeros, (1, rest))
    lo_a = jnp.concatenate([-jnp.sin(ang), half, tail], axis=1)
    hi_a = jnp.concatenate([half, jnp.sin(ang), tail], axis=1)
    row = jnp.floor(pos / GRID_W)
    col = pos - row * GRID_W
    ar = angles(row, HEAD_DIM // 2, AXIAL_THETA)
    ac = angles(col, HEAD_DIM // 2, AXIAL_THETA)
    quarter = jnp.tile(zeros, (1, HEAD_DIM // 4))
    cos_b = jnp.concatenate([jnp.cos(ar), jnp.cos(ar), jnp.cos(ac), jnp.cos(ac)], axis=1)
    lo_b = jnp.concatenate([-jnp.sin(ar), quarter, -jnp.sin(ac), quarter], axis=1)
    hi_b = jnp.concatenate([quarter, jnp.sin(ar), quarter, jnp.sin(ac)], axis=1)
    return (cos_a, lo_a, hi_a), (cos_b, lo_b, hi_b)


def _routing_slots(route, t, bm):
    n_asg = t * TOP_K
    flat_e = route[:, :TOP_K].astype(jnp.int32).reshape(-1)
    flat_w = route[:, TOP_K:2 * TOP_K].reshape(-1)
    asg = jnp.arange(n_asg, dtype=jnp.int32)
    _, asg_sorted, w_sorted = lax.sort((flat_e, asg, lax.bitcast_convert_type(flat_w, jnp.int32)),
                                       num_keys=1, is_stable=True)
    experts = jnp.arange(N_EXPERTS, dtype=jnp.int32)
    counts = jnp.sum((flat_e[:, None] == experts[None, :]).astype(jnp.int32), axis=0)
    start = jnp.cumsum(counts) - counts
    padded = (counts + bm - 1) // bm * bm
    padded_end = jnp.cumsum(padded)
    padded_start = padded_end - padded
    n_slots = -(-(n_asg + N_EXPERTS * (bm - 1)) // bm) * bm
    n_blocks = n_slots // bm
    blk_start = jnp.arange(n_blocks, dtype=jnp.int32) * bm
    blk_expert = jnp.minimum(jnp.searchsorted(padded_end, blk_start, side='right'),
                             N_EXPERTS - 1).astype(jnp.int32)
    blk_valid = jnp.clip(padded_start[blk_expert] + counts[blk_expert] - blk_start, 0, bm)
    blk_valid = jnp.where(blk_start < padded_end[-1], blk_valid, 0).astype(jnp.int32)
    slot = jnp.arange(n_slots, dtype=jnp.int32)
    slot_e = jnp.repeat(blk_expert, bm)
    rank = slot - padded_start[slot_e]
    valid = jnp.logical_and(rank < counts[slot_e], slot < padded_end[-1])
    src = jnp.clip(start[slot_e] + rank, 0, n_asg - 1)
    picked = jnp.stack([asg_sorted, w_sorted], axis=1)[src]
    a = picked[:, 0]
    tok = a // TOP_K
    pads_before = slot - start[slot_e] - jnp.minimum(rank, counts[slot_e])
    slot_tok = jnp.where(valid, tok, 0)
    slot_dest = jnp.where(valid, (a % TOP_K) * t + tok, n_asg + pads_before)
    slot_w = jnp.where(valid, lax.bitcast_convert_type(picked[:, 1], F32), 0.0)
    slot_wb = jnp.broadcast_to(slot_w[:, None], (n_slots, LANES))
    return blk_expert, slot_tok, slot_dest, blk_valid, slot_wb


def _tile(n, pref):
    while n % pref:
        pref //= 2
    return pref


def _trunk(x_prompt, x_sample, attn_norm, w_in, b_gate, q_norm, k_norm, w_proj_a, w_proj_b,
           w_out, ffn_norm, ffn_w_gate, ffn_w_up, ffn_w_down, moe_router, moe_w_gate,
           moe_w_up, moe_w_down, final_norm):
    bp, sp, _ = x_prompt.shape
    bs, ss, _ = x_sample.shape
    tp, ts = bp * sp, bs * ss
    t = tp + ts
    depth = w_in.shape[0]
    assert depth == 2, "dense-FFN layer followed by a final MoE layer"
    assert ss % sp == 0 and tp % ss == 0, "blocks of ss tokens must hold whole sequences"
    sets = ((0, tp), (tp, ts))

    seq_unit = math.gcd(sp, ss)
    bm = _tile(seq_unit, 1024)
    bm_small = _tile(seq_unit, 512)
    n_p_blk, n_pb, n_sb = tp // bm, sp // bm, ss // bm

    def pos_blk(i, j):
        return (jnp.where(i < n_p_blk, i % n_pb, (i - n_p_blk) % n_sb), 0)

    tab_a, tab_b = _rope_tables(max(sp, ss))

    row2 = lambda v: v.reshape(1, -1).astype(F32)
    x, xn = _norm_in(x_prompt.reshape(tp, D_MODEL), x_sample.reshape(ts, D_MODEL),
                     row2(attn_norm[0]), bm_small)

    out = None
    for layer in range(depth):
        qk_a, v_a, q_b, k_b, v_b, gates = _proj_in(
            xn, w_in[layer].astype(BF16), tab_a, tab_b, row2(q_norm[layer]), row2(k_norm[layer]),
            row2(b_gate[layer]), pos_blk, bm)

        oa, lse = [], []
        for g in range(A_N_GROUPS):
            o_g, lse_g = _attn_a(qk_a, v_a, g, tp, sp, ss)
            oa.append(o_g)
            lse.append(lse_g)
        ob = _attn_b(q_b, k_b, v_b, tp, sp, ss)

        merged = _merge(oa, lse, ob, gates, w_proj_a[layer].astype(BF16),
                        w_proj_b[layer].astype(BF16), bm_small // 2)
        moe_layer = layer % 2 == 1
        row_spec = pl.BlockSpec((bm_small, D_MODEL), lambda i, j: (i, 0))
        vec_spec = pl.BlockSpec((1, D_MODEL), lambda i, j: (0, 0))
        if moe_layer:
            wr = moe_router[layer // 2].astype(F32).T
            x, xn_moe, route = _matmul(
                merged, w_out[layer].astype(BF16), bm=bm_small, bn=D_MODEL,
                epilogue=_epi_residual_route,
                extras=[x, row2(ffn_norm[layer]), wr],
                extra_specs=[row_spec, vec_spec, pl.BlockSpec((N_EXPERTS, D_MODEL), lambda i, j: (0, 0))],
                out_shape=[jax.ShapeDtypeStruct((t, D_MODEL), F32),
                           jax.ShapeDtypeStruct((t, D_MODEL), F32),
                           jax.ShapeDtypeStruct((t, LANES), F32)],
                out_specs=[row_spec, row_spec, pl.BlockSpec((bm_small, LANES), lambda i, j: (i, 0))],
                name="proj_out_route")
        else:
            x, xn_ffn = _matmul(
                merged, w_out[layer].astype(BF16), bm=bm_small, bn=D_MODEL,
                epilogue=_epi_residual_norm,
                extras=[x, row2(ffn_norm[layer])],
                extra_specs=[row_spec, vec_spec],
                out_shape=[jax.ShapeDtypeStruct((t, D_MODEL), F32),
                           jax.ShapeDtypeStruct((t, D_MODEL), BF16)],
                out_specs=[row_spec, row_spec],
                name="proj_out")

        j = layer // 2
        if layer % 2 == 0:
            x, xn = _ffn(xn_ffn, ffn_w_gate[j].astype(BF16), ffn_w_up[j].astype(BF16),
                         ffn_w_down[j].astype(BF16), x, row2(attn_norm[layer + 1]), bm_small, 512)
        else:
            meta = _routing_slots(route, t, MOE_BM)
            n_rows_out = meta[1].shape[0]
            y = _moe(xn_moe, moe_w_gate[j], moe_w_up[j], moe_w_down[j], *meta, n_rows_out,
                     MOE_BM, MOE_BC)
            out = tuple(_final(x, y, row2(final_norm), row0, rows, t, bm_small)
                        for row0, rows in sets)
    y_prompt, y_sample = out
    return y_prompt.reshape(bp, sp, D_MODEL), y_sample.reshape(bs, ss, D_MODEL)


def kernel(x_prompt, x_sample, attn_norm, w_in, b_gate, q_norm, k_norm, w_proj_a, w_proj_b,
           w_out, ffn_norm, ffn_w_gate, ffn_w_up, ffn_w_down, moe_router, moe_w_gate,
           moe_w_up, moe_w_down, final_norm):
    return _trunk(x_prompt, x_sample, attn_norm, w_in, b_gate, q_norm, k_norm, w_proj_a,
                  w_proj_b, w_out, ffn_norm, ffn_w_gate, ffn_w_up, ffn_w_down, moe_router,
                  moe_w_gate, moe_w_up, moe_w_down, final_norm)
```

```python
import functools
import math

import jax
import jax.numpy as jnp
from jax import lax
from jax.experimental import pallas as pl
from jax.experimental.pallas import tpu as pltpu

F32 = jnp.float32
BF16 = jnp.bfloat16

D_MODEL = 2048
HEAD_DIM = 128
ATTN_SCALE = HEAD_DIM ** -0.5
EPS = 1e-6
NEG = -1e30

A_WINDOWS = (128, 512, 2048)
A_DILATIONS = (1, 4, 16)
A_N_GROUPS = 3
A_HEADS = 4
A_WIDTH = A_HEADS * HEAD_DIM
A_QK = 2 * A_N_GROUPS * A_WIDTH
A_QKV = 3 * A_N_GROUPS * A_WIDTH
A_RADIUS = 64
ROPE_THETA = 500000.0
ROPE_DIMS = HEAD_DIM // 4

B_Q_HEADS = 8
B_KV_HEADS = 2
B_GROUP = B_Q_HEADS // B_KV_HEADS
B_WIDTH = B_Q_HEADS * HEAD_DIM
B_KVW = B_KV_HEADS * HEAD_DIM
B_QKV = B_WIDTH + 2 * B_KVW
AXIAL_THETA = 10000.0
GRID_W = 64

C_GATE = 2 * D_MODEL
N_EXPERTS = 8
TOP_K = 2

V7X_VMEM_BYTES = 64 * 1024 * 1024
VMEM_LIMIT = V7X_VMEM_BYTES - 8 * 1024 * 1024
LANES = 128

LOG2E = math.log2(math.e)


def _cparams(n_axes, vmem=VMEM_LIMIT):
    return pltpu.CompilerParams(dimension_semantics=("arbitrary",) * n_axes,
                                vmem_limit_bytes=vmem)


def _rms(x, g):
    return x * lax.rsqrt(jnp.mean(x * x, axis=-1, keepdims=True) + EPS) * g


def _rope128(x, c, s_lo, s_hi, w):
    return x * c + pltpu.roll(x, LANES - w, 1) * s_lo + pltpu.roll(x, w, 1) * s_hi


def _norm_in_kernel(xp_ref, xs_ref, g_ref, x_ref, xn_ref, *, n_p):
    i = pl.program_id(0)

    def emit(src):
        x = src[...]
        x_ref[...] = x
        xn_ref[...] = _rms(x, g_ref[...]).astype(BF16)

    @pl.when(i < n_p)
    def _():
        emit(xp_ref)

    @pl.when(i >= n_p)
    def _():
        emit(xs_ref)


def _norm_in(xp, xs, g, bm):
    tp, ts = xp.shape[0], xs.shape[0]
    n_p, n_s = tp // bm, ts // bm
    t = tp + ts
    return pl.pallas_call(
        functools.partial(_norm_in_kernel, n_p=n_p),
        grid=(n_p + n_s,),
        in_specs=[pl.BlockSpec((bm, D_MODEL), lambda i: (jnp.minimum(i, n_p - 1), 0)),
                  pl.BlockSpec((bm, D_MODEL), lambda i: (jnp.maximum(i - n_p, 0), 0)),
                  pl.BlockSpec((1, D_MODEL), lambda i: (0, 0))],
        out_specs=[pl.BlockSpec((bm, D_MODEL), lambda i: (i, 0)),
                   pl.BlockSpec((bm, D_MODEL), lambda i: (i, 0))],
        out_shape=[jax.ShapeDtypeStruct((t, D_MODEL), F32),
                   jax.ShapeDtypeStruct((t, D_MODEL), BF16)],
        compiler_params=_cparams(1),
        name="norm_in",
    )(xp, xs, g)


def _mm_kernel(a_ref, w_ref, *refs, epilogue, n_extra):
    acc = jnp.dot(a_ref[...], w_ref[...], preferred_element_type=F32)
    epilogue(acc, refs[:n_extra], refs[n_extra:])


def _matmul(a, w, *, bm, bn, epilogue, extras=(), extra_specs=(), out_shape, out_specs, name):
    m, k = a.shape
    n = w.shape[1]
    assert m % bm == 0 and n % bn == 0, (m, n, bm, bn)
    return pl.pallas_call(
        functools.partial(_mm_kernel, epilogue=epilogue, n_extra=len(extras)),
        grid=(m // bm, n // bn),
        in_specs=[pl.BlockSpec((bm, k), lambda i, j: (i, 0)),
                  pl.BlockSpec((k, bn), lambda i, j: (0, j)),
                  *extra_specs],
        out_specs=out_specs,
        out_shape=out_shape,
        compiler_params=_cparams(2),
        name=name,
    )(a, w, *extras)


def _epi_residual_norm(acc, extras, outs):
    x = extras[0][...] + acc
    outs[0][...] = x
    outs[1][...] = _rms(x, extras[1][...]).astype(outs[1].dtype)


PROJ_BN = 2 * B_KVW
_J_VA = A_QK // PROJ_BN
_J_QB = A_QKV // PROJ_BN
_J_KV = (A_QKV + B_WIDTH) // PROJ_BN
_J_GATE = (A_QKV + B_QKV) // PROJ_BN
_J_END = (A_QKV + B_QKV + C_GATE) // PROJ_BN


def _proj_in_kernel(xn_ref, w_ref, ca_ref, la_ref, ha_ref, cb_ref, lb_ref, hb_ref,
                    gq_ref, gk_ref, bias_ref,
                    qka_ref, va_ref, qb_ref, kb_ref, vb_ref, gate_ref, acc_ref):
    j = pl.program_id(1)
    p = j - 1

    def matmul():
        acc_ref[...] = jnp.dot(xn_ref[...], w_ref[...], preferred_element_type=F32)

    def heads(n):
        return [slice(h * LANES, (h + 1) * LANES) for h in range(n // LANES)]

    def norm_rope_b(x, g_ref):
        return _rope128(_rms(x, g_ref[...]), cb_ref[...], lb_ref[...], hb_ref[...], HEAD_DIM // 4)

    def epi_qk_a():
        for sl in heads(PROJ_BN):
            qka_ref[:, sl] = _rope128(acc_ref[:, sl], ca_ref[...], la_ref[...], ha_ref[...],
                                      ROPE_DIMS // 2)

    def epi_v_a():
        va_ref[...] = acc_ref[...]

    def epi_q_b():
        tq = qb_ref.shape[2]
        for h, sl in enumerate(heads(PROJ_BN)):
            yt = (norm_rope_b(acc_ref[:, sl], gq_ref) * (ATTN_SCALE * LOG2E)).T
            for n in range(qb_ref.shape[0]):
                qb_ref[n, h * HEAD_DIM:(h + 1) * HEAD_DIM, :] = (
                    yt[:, n * tq:(n + 1) * tq].astype(BF16))

    def epi_kv_b():
        for sl in heads(B_KVW):
            kb_ref[:, sl] = norm_rope_b(acc_ref[:, sl], gk_ref).astype(BF16)
        tk = vb_ref.shape[2]
        vt = acc_ref[:, B_KVW:].T
        for n in range(vb_ref.shape[0]):
            vb_ref[n] = vt[:, n * tk:(n + 1) * tk].astype(BF16)

    def epi_gate():
        gate_ref[...] = (acc_ref[...] + bias_ref[...]).astype(BF16)

    def between(lo, hi):
        return jnp.logical_and(p >= lo, p < hi)

    @pl.when(j == 0)
    def _():
        matmul()

    for lo, hi, epilogue in ((0, _J_VA, epi_qk_a), (_J_VA, _J_QB, epi_v_a), (_J_QB, _J_KV, epi_q_b),
                             (_J_KV, _J_GATE, epi_kv_b), (_J_GATE, _J_END - 1, epi_gate)):
        @pl.when(between(lo, hi))
        def _(epilogue=epilogue):
            epilogue()
            matmul()

    @pl.when(p == _J_END - 1)
    def _():
        epi_gate()


def _proj_in(xn, w, tab_a, tab_b, gq, gk, bias, pos_blk, bm):
    t = xn.shape[0]
    bn = PROJ_BN
    tq, tk = ATTN_B_TQ, ATTN_B_TK
    assert bm % tq == 0 and bm % tk == 0

    def cols(lo, hi):
        return lambda i, j: (i, jnp.clip(j - 1 - lo, 0, hi - lo - 1))

    tab_spec = pl.BlockSpec((bm, LANES), pos_blk)
    vec_spec = pl.BlockSpec((1, LANES), lambda i, j: (0, 0))
    return pl.pallas_call(
        _proj_in_kernel,
        grid=(t // bm, _J_END + 1),
        in_specs=[pl.BlockSpec((bm, D_MODEL), lambda i, j: (i, 0)),
                  pl.BlockSpec((D_MODEL, bn), lambda i, j: (0, jnp.minimum(j, _J_END - 1))),
                  *([tab_spec] * 6), vec_spec, vec_spec,
                  pl.BlockSpec((1, bn), lambda i, j: (0, jnp.clip(j - 1 - _J_GATE, 0, _J_END - _J_GATE - 1)))],
        out_specs=[pl.BlockSpec((bm, bn), cols(0, _J_VA)),
                   pl.BlockSpec((bm, bn), cols(_J_VA, _J_QB)),
                   pl.BlockSpec((bm // tq, bn, tq),
                                lambda i, j: (i, jnp.clip(j - 1 - _J_QB, 0, _J_KV - _J_QB - 1), 0)),
                   pl.BlockSpec((bm, B_KVW), lambda i, j: (i, 0)),
                   pl.BlockSpec((bm // tk, B_KVW, tk), lambda i, j: (i, 0, 0)),
                   pl.BlockSpec((bm, bn), cols(_J_GATE, _J_END))],
        out_shape=[jax.ShapeDtypeStruct((t, A_QK), F32),
                   jax.ShapeDtypeStruct((t, A_QKV - A_QK), F32),
                   jax.ShapeDtypeStruct((t // tq, B_WIDTH, tq), BF16),
                   jax.ShapeDtypeStruct((t, B_KVW), BF16),
                   jax.ShapeDtypeStruct((t // tk, B_KVW, tk), BF16),
                   jax.ShapeDtypeStruct((t, C_GATE), BF16)],
        scratch_shapes=[pltpu.VMEM((bm, bn), F32)],
        compiler_params=_cparams(2),
        name="proj_in",
    )(xn, w, *tab_a, *tab_b, gq, gk, bias)


ATTN_A_UNROLL = 8
ATTN_A_TQ = 256
ATTN_A_WHOLE = 512


def _attn_a_tiles(l_sub):
    if l_sub <= ATTN_A_WHOLE:
        return l_sub, l_sub
    assert l_sub % ATTN_A_TQ == 0
    return ATTN_A_TQ, ATTN_A_TQ + 2 * A_RADIUS


def _band_bias(tq, win):
    r = jnp.arange(tq, dtype=jnp.int32)[:, None]
    c = jnp.arange(win, dtype=jnp.int32)[None, :]
    offs = (0, A_RADIUS, win - tq)
    return jnp.stack([jnp.where(jnp.abs(c - r - off) <= A_RADIUS, 0.0, NEG).astype(F32)
                      for off in offs])


def _attn_a_kernel(q_ref, k_ref, v_ref, bias_p_ref, bias_s_ref, o_ref, lse_ref, *,
                   d, n_short_blk, sp, ss):
    def rows(start, size):
        if d == 1:
            return pl.ds(start, size)
        return pl.ds(start, size, stride=d)

    def run(seq, n_seq, bias_ref):
        l_sub = seq // d
        tq, win = _attn_a_tiles(l_sub)
        n_tile = l_sub // tq

        def tile(idx, carry):
            base = (idx // (d * n_tile)) * seq
            r = (idx // n_tile) % d
            ti = idx % n_tile
            t0 = ti * tq
            ks = jnp.clip(t0 - A_RADIUS, 0, l_sub - win)
            if n_tile == 1:
                bias = bias_ref[0]
            else:
                bias = bias_ref[jnp.where(ti == 0, 0, jnp.where(ti == n_tile - 1, 2, 1))]
            q = (q_ref[rows(base + r + t0 * d, tq), :] * ATTN_SCALE).astype(BF16)
            k = k_ref[rows(base + r + ks * d, win), :].astype(BF16)
            v = v_ref[rows(base + r + ks * d, win), :].astype(BF16)
            s = lax.dot_general(q, k, (((1,), (1,)), ((), ())), preferred_element_type=F32) + bias
            m = jnp.max(s, axis=-1, keepdims=True)
            p = jnp.exp(s - m)
            den = jnp.sum(p, axis=-1, keepdims=True)
            o = jnp.dot(p.astype(BF16), v, preferred_element_type=F32) / den
            lse = m + jnp.log(den)
            o_ref[rows(base + r + t0 * d, tq), :] = o
            lse_ref[rows(base + r + t0 * d, tq), :] = jnp.broadcast_to(lse, (tq, LANES))
            return carry

        lax.fori_loop(0, n_seq * d * n_tile, tile, 0, unroll=ATTN_A_UNROLL)

    if sp == ss:
        run(ss, 1, bias_s_ref)
    else:
        blk = pl.program_id(0)

        @pl.when(blk < n_short_blk)
        def _():
            run(sp, ss // sp, bias_p_ref)

        @pl.when(blk >= n_short_blk)
        def _():
            run(ss, 1, bias_s_ref)


def _attn_a(qk, v, g, tp, sp, ss):
    t = qk.shape[0]
    d = A_DILATIONS[g]
    biases = [_band_bias(*_attn_a_tiles(seq // d)) for seq in (sp, ss)]
    kern = functools.partial(_attn_a_kernel, d=d, n_short_blk=tp // ss, sp=sp, ss=ss)
    return pl.pallas_call(
        kern,
        grid=(t // ss, A_HEADS),
        in_specs=[pl.BlockSpec((ss, LANES), lambda b, h: (b, g * A_HEADS + h)),
                  pl.BlockSpec((ss, LANES), lambda b, h: (b, (A_N_GROUPS + g) * A_HEADS + h)),
                  pl.BlockSpec((ss, LANES), lambda b, h: (b, g * A_HEADS + h)),
                  *[pl.BlockSpec(bias.shape, lambda b, h: (0, 0, 0)) for bias in biases]],
        out_specs=[pl.BlockSpec((ss, LANES), lambda b, h: (b, h))] * 2,
        out_shape=[jax.ShapeDtypeStruct((t, A_WIDTH), F32)] * 2,
        compiler_params=_cparams(2),
        name=f"attn_a{g}",
    )(qk, qk, v, *biases)


ATTN_B_TQ = 512
ATTN_B_TK = 1024


def _attn_b_kernel(qt_ref, qt_next_ref, k_ref, vt_ref, o_ref, s_ref, *, tq, tk, tp, sp, ss):
    i = pl.program_id(1)
    tiles_per_block = ss // tq

    def first_chunk(row):
        return jnp.where(row < tp, ((row % ss) // sp) * sp, 0) // tk

    row = i * tq
    first = first_chunk(row)
    n_chunk = jnp.where(row < tp, sp // tk, ss // tk)
    last = first + n_chunk - 1

    def heads_on_lanes(ref):
        return jnp.concatenate([ref[h * HEAD_DIM:(h + 1) * HEAD_DIM, :] for h in range(B_GROUP)],
                               axis=1)

    qt = heads_on_lanes(qt_ref)
    cols = B_GROUP * tq

    def scores(c, slot, q):
        k = k_ref[pl.ds(pl.multiple_of(c * tk, tk), tk), :]
        s_ref[slot] = jnp.dot(k, q, preferred_element_type=F32)

    def chunk(c, slot, carry, prefetch):
        prefetch()
        m, l, acc = carry
        st = s_ref[slot]
        m_new = jnp.maximum(m, jnp.max(st, axis=0, keepdims=True))
        alpha = jnp.exp2(m - m_new)
        pt = jnp.exp2(st - m_new)
        l = alpha * l + jnp.sum(pt, axis=0, keepdims=True)
        acc = alpha * acc + jnp.dot(vt_ref[c], pt.astype(BF16), preferred_element_type=F32)
        return m_new, l, acc

    def body(it, carry):
        for slot in range(2):
            c = first + 2 * it + slot
            carry = chunk(c, slot, carry, lambda c=c, slot=slot: scores(c + 1, 1 - slot, qt))
        return carry

    @pl.when(i % tiles_per_block == 0)
    def _():
        scores(first, 0, qt)

    init = (jnp.full((1, cols), NEG, F32), jnp.zeros((1, cols), F32),
            jnp.zeros((HEAD_DIM, cols), F32))
    carry = lax.fori_loop(0, n_chunk // 2 - 1, body, init)
    carry = chunk(last - 1, 0, carry, lambda: scores(last, 1, qt))
    next_first = first_chunk(row + tq)
    _, l, acc = chunk(last, 1, carry,
                      lambda: scores(next_first, 0, heads_on_lanes(qt_next_ref)))
    ot = acc / l
    for h in range(B_GROUP):
        o_ref[:, h * LANES:(h + 1) * LANES] = ot[:, h * tq:(h + 1) * tq].T.astype(o_ref.dtype)


def _attn_b(qt, k, vt, tp, sp, ss):
    tq, tk = ATTN_B_TQ, ATTN_B_TK
    t = k.shape[0]
    n_tile = t // tq
    gw = B_GROUP * HEAD_DIM
    assert sp % (2 * tk) == 0 and ss % (2 * tk) == 0 and sp % tq == 0
    return pl.pallas_call(
        functools.partial(_attn_b_kernel, tq=tq, tk=tk, tp=tp, sp=sp, ss=ss),
        grid=(B_KV_HEADS, n_tile),
        scratch_shapes=[pltpu.VMEM((2, tk, gw // HEAD_DIM * tq), F32)],
        in_specs=[pl.BlockSpec((None, gw, tq), lambda kv, i: (i, kv, 0)),
                  pl.BlockSpec((None, gw, tq), lambda kv, i: (jnp.minimum(i + 1, n_tile - 1), kv, 0)),
                  pl.BlockSpec((ss, HEAD_DIM), lambda kv, i: (i // (ss // tq), kv)),
                  pl.BlockSpec((ss // tk, HEAD_DIM, tk), lambda kv, i: (i // (ss // tq), kv, 0))],
        out_specs=pl.BlockSpec((tq, gw), lambda kv, i: (i, kv)),
        out_shape=jax.ShapeDtypeStruct((t, B_WIDTH), BF16),
        compiler_params=_cparams(2),
        name="attn_b",
    )(qt, qt, k, vt)


def _merge_kernel(o0, o1, o2, l0, l1, l2, ob_ref, gate_ref, wpa_ref, wpb_ref, out_ref):
    lse = [l0[...], l1[...], l2[...]]
    top = jnp.maximum(jnp.maximum(lse[0], lse[1]), lse[2])
    e = [jnp.exp(x - top) for x in lse]
    oa = (e[0] * o0[...] + e[1] * o1[...] + e[2] * o2[...]) / (e[0] + e[1] + e[2])
    pa = jnp.dot(oa.astype(BF16), wpa_ref[...], preferred_element_type=F32)
    pb = jnp.dot(ob_ref[...], wpb_ref[...], preferred_element_type=F32)
    ga = jax.nn.sigmoid(gate_ref[:, :D_MODEL].astype(F32))
    gb = jax.nn.sigmoid(gate_ref[:, D_MODEL:].astype(F32))
    out_ref[...] = (ga * pa + gb * pb).astype(out_ref.dtype)


def _merge(oa, lse, ob, gates, wpa, wpb, bm):
    t = ob.shape[0]
    row = lambda w: pl.BlockSpec((bm, w), lambda i: (i, 0))
    const = lambda s: pl.BlockSpec(s, lambda i: (0, 0))
    return pl.pallas_call(
        _merge_kernel,
        grid=(t // bm,),
        in_specs=[row(A_WIDTH)] * 6 + [row(B_WIDTH), row(C_GATE),
                                       const((A_WIDTH, D_MODEL)), const((B_WIDTH, D_MODEL))],
        out_specs=row(D_MODEL),
        out_shape=jax.ShapeDtypeStruct((t, D_MODEL), BF16),
        compiler_params=_cparams(1),
        name="merge",
    )(*oa, *lse, ob, gates, wpa, wpb)


def _ffn_kernel(xn_ref, wg_ref, wu_ref, wd_ref, x_ref, gn_ref, o_ref, on_ref, *, n_chunk):
    c = pl.program_id(1)

    @pl.when(c == 0)
    def _():
        o_ref[...] = x_ref[...]

    xn = xn_ref[...]
    g = jnp.dot(xn, wg_ref[...], preferred_element_type=F32)
    u = jnp.dot(xn, wu_ref[...], preferred_element_type=F32)
    hm = (g * jax.nn.sigmoid(g) * u).astype(BF16)
    o_ref[...] += jnp.dot(hm, wd_ref[...], preferred_element_type=F32)

    @pl.when(c == n_chunk - 1)
    def _():
        on_ref[...] = _rms(o_ref[...], gn_ref[...]).astype(on_ref.dtype)


def _ffn(xn, wg, wu, wd, x, g_next, bm, bc):
    t = xn.shape[0]
    f = wg.shape[1]
    n_chunk = f // bc
    row = lambda: pl.BlockSpec((bm, D_MODEL), lambda i, c: (i, 0))
    return pl.pallas_call(
        functools.partial(_ffn_kernel, n_chunk=n_chunk),
        grid=(t // bm, n_chunk),
        in_specs=[row(),
                  pl.BlockSpec((D_MODEL, bc), lambda i, c: (0, c)),
                  pl.BlockSpec((D_MODEL, bc), lambda i, c: (0, c)),
                  pl.BlockSpec((bc, D_MODEL), lambda i, c: (c, 0)),
                  row(),
                  pl.BlockSpec((1, D_MODEL), lambda i, c: (0, 0))],
        out_specs=[row(), row()],
        out_shape=[jax.ShapeDtypeStruct((t, D_MODEL), F32),
                   jax.ShapeDtypeStruct((t, D_MODEL), BF16)],
        compiler_params=_cparams(2),
        name="ffn_dense",
    )(xn, wg, wu, wd, x, g_next)


def _route(xn, wrt_ref):
    logits = [jnp.sum(xn * wrt_ref[e:e + 1, :], axis=-1, keepdims=True) for e in range(N_EXPERTS)]
    v1, i1 = logits[0], jnp.zeros(logits[0].shape, jnp.int32)
    for e in range(1, N_EXPERTS):
        better = logits[e] > v1
        i1 = jnp.where(better, e, i1)
        v1 = jnp.where(better, logits[e], v1)
    v2 = jnp.full(v1.shape, -jnp.inf, F32)
    i2 = jnp.zeros(i1.shape, jnp.int32)
    for e in range(N_EXPERTS):
        cand = jnp.where(i1 == e, -jnp.inf, logits[e])
        better = cand > v2
        i2 = jnp.where(better, e, i2)
        v2 = jnp.where(better, cand, v2)
    e2 = jnp.exp(v2 - v1)
    w1 = 1.0 / (1.0 + e2)
    w2 = e2 / (1.0 + e2)
    lane = lax.broadcasted_iota(jnp.int32, (xn.shape[0], LANES), 1)
    return jnp.where(lane == 0, i1.astype(F32),
                     jnp.where(lane == 1, i2.astype(F32),
                               jnp.where(lane == 2, w1, jnp.where(lane == 3, w2, 0.0))))


def _epi_residual_route(acc, extras, outs):
    x = extras[0][...] + acc
    xn = _rms(x, extras[1][...])
    outs[0][...] = x
    outs[1][...] = xn
    outs[2][...] = _route(xn, extras[2])


MOE_BM = 672
MOE_BC = 512


def _moe_kernel(be_ref, nvalid_ref,
                tok0_ref, tok_ref, dest_ref, xn_hbm, wg_ref, wu_ref, wd_ref, y_hbm,
                xs_f32, xs_bf, acc, sem_in, sem_out, *, bm, n_chunk, n_blocks):
    i = pl.program_id(0)
    c = pl.program_id(1)
    rows_per_step = bm // n_chunk
    is_block = i < n_blocks
    used = jnp.logical_and(is_block, nvalid_ref[jnp.minimum(i, n_blocks - 1)] > 0)
    gathers_next = i + 1 < n_blocks
    scatters_prev = i >= 1
    cur = i % 2

    def row_in(r, t):
        return pltpu.make_async_copy(xn_hbm.at[pl.ds(t, 1), :], xs_f32.at[pl.ds(r, 1), :], sem_in)

    def row_out(slot, r, t):
        return pltpu.make_async_copy(acc.at[slot, pl.ds(r, 1), :], y_hbm.at[pl.ds(t, 1), :], sem_out)

    def wait_all_in():
        pltpu.make_async_copy(xn_hbm.at[pl.ds(0, bm), :], xs_f32, sem_in).wait()

    def wait_all_out():
        pltpu.make_async_copy(acc.at[0], y_hbm.at[pl.ds(0, bm), :], sem_out).wait()

    def issue_gather():
        for k in range(rows_per_step):
            r = c * rows_per_step + k
            row_in(r, tok_ref[0, r]).start()

    def issue_scatter():
        for k in range(rows_per_step):
            r = c * rows_per_step + k
            row_out(1 - cur, r, dest_ref[0, r]).start()

    def compute():
        xs = xs_bf[...]
        g = jnp.dot(xs, wg_ref[...].astype(BF16), preferred_element_type=F32)
        u = jnp.dot(xs, wu_ref[...].astype(BF16), preferred_element_type=F32)
        hm = (g * jax.nn.sigmoid(g) * u).astype(BF16)
        part = jnp.dot(hm, wd_ref[...].astype(BF16), preferred_element_type=F32)
        acc[cur] += part

    @pl.when(c == 0)
    def _():
        @pl.when(i == 0)
        def _():
            def start(r, carry):
                row_in(r, tok0_ref[0, r]).start()
                return carry
            lax.fori_loop(0, bm, start, 0)

        @pl.when(is_block)
        def _():
            wait_all_in()
            xs_bf[...] = xs_f32[...].astype(BF16)

        @pl.when(i >= 2)
        def _():
            wait_all_out()

        @pl.when(used)
        def _():
            acc[cur] = jnp.zeros((bm, D_MODEL), F32)

    fast = jnp.logical_and(jnp.logical_and(used, gathers_next), scatters_prev)

    @pl.when(fast)
    def _():
        issue_gather()
        issue_scatter()
        compute()

    @pl.when(jnp.logical_not(fast))
    def _():
        @pl.when(gathers_next)
        def _():
            issue_gather()

        @pl.when(scatters_prev)
        def _():
            issue_scatter()

        @pl.when(used)
        def _():
            compute()

    @pl.when(jnp.logical_and(i == n_blocks, c == n_chunk - 1))
    def _():
        wait_all_out()


def _moe(xn, wg, wu, wd, blk_expert, slot_tok, slot_dest, blk_valid, n_rows_out, bm, bc):
    n_slots = slot_tok.shape[0]
    n_blocks = n_slots // bm
    f = wg.shape[2]
    n_chunk = f // bc
    assert bm % n_chunk == 0 and n_blocks >= 2
    last = n_blocks - 1
    tok0_spec = pl.BlockSpec((None, 1, bm), lambda i, c, *_: (0, 0, 0), memory_space=pltpu.SMEM)
    tok_spec = pl.BlockSpec((None, 1, bm), lambda i, c, *_: (jnp.minimum(i + 1, last), 0, 0),
                            memory_space=pltpu.SMEM)
    dest_spec = pl.BlockSpec((None, 1, bm), lambda i, c, *_: (jnp.maximum(i - 1, 0), 0, 0),
                             memory_space=pltpu.SMEM)
    blk = lambda i: jnp.minimum(i, last)

    def wcol(i, c, nv):
        return jnp.where(jnp.logical_and(i <= last, nv[blk(i)] > 0), c, n_chunk - 1)

    grid_spec = pltpu.PrefetchScalarGridSpec(
        num_scalar_prefetch=2,
        grid=(n_blocks + 1, n_chunk),
        in_specs=[tok0_spec, tok_spec, dest_spec,
                  pl.BlockSpec(memory_space=pl.ANY),
                  pl.BlockSpec((None, D_MODEL, bc), lambda i, c, be, nv: (be[blk(i)], 0, wcol(i, c, nv))),
                  pl.BlockSpec((None, D_MODEL, bc), lambda i, c, be, nv: (be[blk(i)], 0, wcol(i, c, nv))),
                  pl.BlockSpec((None, bc, D_MODEL), lambda i, c, be, nv: (be[blk(i)], wcol(i, c, nv), 0))],
        out_specs=pl.BlockSpec(memory_space=pl.ANY),
        scratch_shapes=[pltpu.VMEM((bm, D_MODEL), F32),
                        pltpu.VMEM((bm, D_MODEL), BF16),
                        pltpu.VMEM((2, bm, D_MODEL), F32),
                        pltpu.SemaphoreType.DMA(()),
                        pltpu.SemaphoreType.DMA(())],
    )
    return pl.pallas_call(
        functools.partial(_moe_kernel, bm=bm, n_chunk=n_chunk, n_blocks=n_blocks),
        grid_spec=grid_spec,
        out_shape=jax.ShapeDtypeStruct((n_rows_out, D_MODEL), F32),
        compiler_params=_cparams(2),
        name="moe_experts",
    )(blk_expert, blk_valid, slot_tok.reshape(n_blocks, 1, bm), slot_tok.reshape(n_blocks, 1, bm),
      slot_dest.reshape(n_blocks, 1, bm), xn, wg, wu, wd)


def _final_kernel(x_ref, y0_ref, y1_ref, route_ref, g_ref, o_ref):
    w0 = route_ref[:, TOP_K:TOP_K + 1]
    w1 = route_ref[:, TOP_K + 1:TOP_K + 2]
    x = x_ref[...] + (y0_ref[...] * w0 + y1_ref[...] * w1)
    o_ref[...] = _rms(x, g_ref[...])


def _final(x, y, route, g, row0, rows, t, bm):
    rb, tb = row0 // bm, t // bm
    return pl.pallas_call(
        _final_kernel,
        grid=(rows // bm,),
        in_specs=[pl.BlockSpec((bm, D_MODEL), lambda i: (rb + i, 0)),
                  pl.BlockSpec((bm, D_MODEL), lambda i: (rb + i, 0)),
                  pl.BlockSpec((bm, D_MODEL), lambda i: (tb + rb + i, 0)),
                  pl.BlockSpec((bm, LANES), lambda i: (rb + i, 0)),
                  pl.BlockSpec((1, D_MODEL), lambda i: (0, 0))],
        out_specs=pl.BlockSpec((bm, D_MODEL), lambda i: (i, 0)),
        out_shape=jax.ShapeDtypeStruct((rows, D_MODEL), F32),
        compiler_params=_cparams(1),
        name="final_norm",
    )(x, y, y, route, g)


def _rope_tables(n_pos):
    pos = jnp.arange(n_pos, dtype=F32)
    ones = jnp.ones((n_pos, 1), F32)
    zeros = jnp.zeros((n_pos, 1), F32)

    def angles(p, dims, theta):
        inv = jnp.power(jnp.float32(theta), -jnp.arange(0, dims, 2, dtype=F32) / dims)
        return p[:, None] * inv[None, :]

    ang = angles(pos, ROPE_DIMS, ROPE_THETA)
    rest = HEAD_DIM - ROPE_DIMS
    cos_a = jnp.concatenate([jnp.cos(ang), jnp.cos(ang), jnp.tile(ones, (1, rest))], axis=1)
    half = jnp.tile(zeros, (1, ROPE_DIMS // 2))
    tail = jnp.tile(zeros, (1, rest))
    lo_a = jnp.concatenate([-jnp.sin(ang), half, tail], axis=1)
    hi_a = jnp.concatenate([half, jnp.sin(ang), tail], axis=1)
    row = jnp.floor(pos / GRID_W)
    col = pos - row * GRID_W
    ar = angles(row, HEAD_DIM // 2, AXIAL_THETA)
    ac = angles(col, HEAD_DIM // 2, AXIAL_THETA)
    quarter = jnp.tile(zeros, (1, HEAD_DIM // 4))
    cos_b = jnp.concatenate([jnp.cos(ar), jnp.cos(ar), jnp.cos(ac), jnp.cos(ac)], axis=1)
    lo_b = jnp.concatenate([-jnp.sin(ar), quarter, -jnp.sin(ac), quarter], axis=1)
    hi_b = jnp.concatenate([quarter, jnp.sin(ar), quarter, jnp.sin(ac)], axis=1)
    return (cos_a, lo_a, hi_a), (cos_b, lo_b, hi_b)


def _routing_slots(route, t, bm):
    n_asg = t * TOP_K
    flat_e = route[:, :TOP_K].astype(jnp.int32).reshape(-1)
    asg = jnp.arange(n_asg, dtype=jnp.int32)
    _, asg_sorted = lax.sort((flat_e, asg), num_keys=1, is_stable=True)
    experts = jnp.arange(N_EXPERTS, dtype=jnp.int32)
    counts = jnp.sum((flat_e[:, None] == experts[None, :]).astype(jnp.int32), axis=0)
    start = jnp.cumsum(counts) - counts
    padded = (counts + bm - 1) // bm * bm
    padded_end = jnp.cumsum(padded)
    padded_start = padded_end - padded
    n_slots = -(-(n_asg + N_EXPERTS * (bm - 1)) // bm) * bm
    n_blocks = n_slots // bm
    blk_start = jnp.arange(n_blocks, dtype=jnp.int32) * bm
    blk_expert = jnp.minimum(jnp.searchsorted(padded_end, blk_start, side='right'),
                             N_EXPERTS - 1).astype(jnp.int32)
    blk_valid = jnp.clip(padded_start[blk_expert] + counts[blk_expert] - blk_start, 0, bm)
    blk_valid = jnp.where(blk_start < padded_end[-1], blk_valid, 0).astype(jnp.int32)
    slot = jnp.arange(n_slots, dtype=jnp.int32)
    slot_e = jnp.repeat(blk_expert, bm)
    rank = slot - padded_start[slot_e]
    valid = jnp.logical_and(rank < counts[slot_e], slot < padded_end[-1])
    src = jnp.clip(start[slot_e] + rank, 0, n_asg - 1)
    a = asg_sorted[src]
    tok = a // TOP_K
    pads_before = slot - start[slot_e] - jnp.minimum(rank, counts[slot_e])
    slot_tok = jnp.where(valid, tok, 0)
    slot_dest = jnp.where(valid, (a % TOP_K) * t + tok, n_asg + pads_before)
    return blk_expert, slot_tok, slot_dest, blk_valid


def _tile(n, pref):
    while n % pref:
        pref //= 2
    return pref


def _trunk(x_prompt, x_sample, attn_norm, w_in, b_gate, q_norm, k_norm, w_proj_a, w_proj_b,
           w_out, ffn_norm, ffn_w_gate, ffn_w_up, ffn_w_down, moe_router, moe_w_gate,
           moe_w_up, moe_w_down, final_norm):
    bp, sp, _ = x_prompt.shape
    bs, ss, _ = x_sample.shape
    tp, ts = bp * sp, bs * ss
    t = tp + ts
    depth = w_in.shape[0]
    assert depth == 2, "dense-FFN layer followed by a final MoE layer"
    assert ss % sp == 0 and tp % ss == 0, "blocks of ss tokens must hold whole sequences"
    sets = ((0, tp), (tp, ts))

    seq_unit = math.gcd(sp, ss)
    bm = _tile(seq_unit, 1024)
    bm_small = _tile(seq_unit, 512)
    n_p_blk, n_pb, n_sb = tp // bm, sp // bm, ss // bm

    def pos_blk(i, j):
        return (jnp.where(i < n_p_blk, i % n_pb, (i - n_p_blk) % n_sb), 0)

    tab_a, tab_b = _rope_tables(max(sp, ss))

    row2 = lambda v: v.reshape(1, -1).astype(F32)
    x, xn = _norm_in(x_prompt.reshape(tp, D_MODEL), x_sample.reshape(ts, D_MODEL),
                     row2(attn_norm[0]), bm_small)

    out = None
    for layer in range(depth):
        qk_a, v_a, q_b, k_b, v_b, gates = _proj_in(
            xn, w_in[layer].astype(BF16), tab_a, tab_b, row2(q_norm[layer]), row2(k_norm[layer]),
            row2(b_gate[layer]), pos_blk, bm)

        oa, lse = [], []
        for g in range(A_N_GROUPS):
            o_g, lse_g = _attn_a(qk_a, v_a, g, tp, sp, ss)
            oa.append(o_g)
            lse.append(lse_g)
        ob = _attn_b(q_b, k_b, v_b, tp, sp, ss)

        merged = _merge(oa, lse, ob, gates, w_proj_a[layer].astype(BF16),
                        w_proj_b[layer].astype(BF16), bm_small // 2)
        moe_layer = layer % 2 == 1
        row_spec = pl.BlockSpec((bm_small, D_MODEL), lambda i, j: (i, 0))
        vec_spec = pl.BlockSpec((1, D_MODEL), lambda i, j: (0, 0))
        if moe_layer:
            wr = moe_router[layer // 2].astype(F32).T
            x, xn_moe, route = _matmul(
                merged, w_out[layer].astype(BF16), bm=bm_small, bn=D_MODEL,
                epilogue=_epi_residual_route,
                extras=[x, row2(ffn_norm[layer]), wr],
                extra_specs=[row_spec, vec_spec, pl.BlockSpec((N_EXPERTS, D_MODEL), lambda i, j: (0, 0))],
                out_shape=[jax.ShapeDtypeStruct((t, D_MODEL), F32),
                           jax.ShapeDtypeStruct((t, D_MODEL), F32),
                           jax.ShapeDtypeStruct((t, LANES), F32)],
                out_specs=[row_spec, row_spec, pl.BlockSpec((bm_small, LANES), lambda i, j: (i, 0))],
                name="proj_out_route")
        else:
            x, xn_ffn = _matmul(
                merged, w_out[layer].astype(BF16), bm=bm_small, bn=D_MODEL,
                epilogue=_epi_residual_norm,
                extras=[x, row2(ffn_norm[layer])],
                extra_specs=[row_spec, vec_spec],
                out_shape=[jax.ShapeDtypeStruct((t, D_MODEL), F32),
                           jax.ShapeDtypeStruct((t, D_MODEL), BF16)],
                out_specs=[row_spec, row_spec],
                name="proj_out")

        j = layer // 2
        if layer % 2 == 0:
            x, xn = _ffn(xn_ffn, ffn_w_gate[j].astype(BF16), ffn_w_up[j].astype(BF16),
                         ffn_w_down[j].astype(BF16), x, row2(attn_norm[layer + 1]), bm_small, 512)
        else:
            meta = _routing_slots(route, t, MOE_BM)
            n_rows_out = meta[1].shape[0]
            y = _moe(xn_moe, moe_w_gate[j], moe_w_up[j], moe_w_down[j], *meta, n_rows_out,
                     MOE_BM, MOE_BC)
            out = tuple(_final(x, y, route, row2(final_norm), row0, rows, t, bm_small)
                        for row0, rows in sets)
    y_prompt, y_sample = out
    return y_prompt.reshape(bp, sp, D_MODEL), y_sample.reshape(bs, ss, D_MODEL)


def kernel(x_prompt, x_sample, attn_norm, w_in, b_gate, q_norm, k_norm, w_proj_a, w_proj_b,
           w_out, ffn_norm, ffn_w_gate, ffn_w_up, ffn_w_down, moe_router, moe_w_gate,
           moe_w_up, moe_w_down, final_norm):
    return _trunk(x_prompt, x_sample, attn_norm, w_in, b_gate, q_norm, k_norm, w_proj_a,
                  w_proj_b, w_out, ffn_norm, ffn_w_gate, ffn_w_up, ffn_w_down, moe_router,
                  moe_w_gate, moe_w_up, moe_w_down, final_norm)
```
